```python
import math
import jax, jax.numpy as jnp
from jax import lax
import numpy as np

D_MODEL = 1024
BATCH = 4
SEQ = 4096
DEPTH = 4
DEC_BATCH = 32
DEC_SEQ = 4
PAST_LEN = 8192
PAGE_SIZE = 128

F32 = jnp.float32
MIX_WIDTH = D_MODEL
GROUP_WIDTH = MIX_WIDTH // 4
S5_WIDTH = GROUP_WIDTH
S5_GROUP = 16
S5_GROUPS = S5_WIDTH // S5_GROUP
S5_STATE = 64
FOX_WIDTH = GROUP_WIDTH
FOX_HEAD_DIM = 64
FOX_HEADS = FOX_WIDTH // FOX_HEAD_DIM
FOX_SCALE = FOX_HEAD_DIM ** -0.5
Q_BLOCK = 128
SSD_WIDTH = GROUP_WIDTH
SSD_HEAD_DIM = 64
SSD_HEADS = SSD_WIDTH // SSD_HEAD_DIM
SSD_GROUPS = 2
SSD_STATE = 128
SSD_CONV = 4
SSD_CHUNK = 128
SSD_XBC = SSD_WIDTH + 2 * SSD_GROUPS * SSD_STATE
SC_WIDTH = GROUP_WIDTH
SC_CONV = 3
FFN_WIDTH = 4 * D_MODEL
EPS = 1e-5
POOL_NUM = 5
POOL_DEN = 4
IN_SPLITS = (S5_WIDTH, FOX_WIDTH, FOX_WIDTH, FOX_WIDTH, FOX_HEADS,
             SSD_WIDTH, SSD_XBC, SSD_HEADS, SC_WIDTH, SC_WIDTH, SC_WIDTH)
IN_COLS = S5_WIDTH + 3 * FOX_WIDTH + FOX_HEADS + SSD_WIDTH + SSD_XBC + SSD_HEADS + 3 * SC_WIDTH

kernel_name = 'hybrid_s5_fox_ssd_shortconv_step'


def rmsnorm(x, g):
    xf = x.astype(F32)
    y = xf * lax.rsqrt(jnp.mean(xf * xf, axis=-1, keepdims=True) + EPS)
    return (y * g.astype(F32)).astype(x.dtype)


def split_cols(proj):
    outs, start = [], 0
    for n in IN_SPLITS:
        outs.append(proj[..., start:start + n])
        start += n
    return outs


def causal_dwconv(x, buf, w):
    width, T = w.shape[0], x.shape[1]
    xp = jnp.concatenate([buf.astype(x.dtype), x], axis=1)
    y = xp[:, 0:T] * w[0]
    for j in range(1, width):
        y = y + xp[:, j:j + T] * w[j]
    return y, xp[:, T:]


def _cplx_combine(e1, e2):
    a1r, a1i, b1r, b1i = e1
    a2r, a2i, b2r, b2i = e2
    return (a2r * a1r - a2i * a1i, a2r * a1i + a2i * a1r,
            a2r * b1r - a2i * b1i + b2r, a2r * b1i + a2i * b1r + b2i)


def s5_mix(u, h0, lam_re, lam_im, log_dt, b_re, b_im, c_re, c_im, d_skip, w_glu):
    bsz, T, _ = u.shape
    ug = u.astype(F32).reshape(bsz, T, S5_GROUPS, S5_GROUP)
    lr, li = lam_re.astype(F32), lam_im.astype(F32)
    dt = jnp.exp(log_dt.astype(F32))[:, None]
    mag = jnp.exp(lr * dt)
    abr, abi = mag * jnp.cos(li * dt), mag * jnp.sin(li * dt)
    den = lr * lr + li * li
    qr = ((abr - 1.0) * lr + abi * li) / den
    qi = (abi * lr - (abr - 1.0) * li) / den
    br, bi = b_re.astype(F32), b_im.astype(F32)
    bbr = qr[..., None] * br - qi[..., None] * bi
    bbi = qr[..., None] * bi + qi[..., None] * br
    bur = jnp.einsum('btgh,gph->btgp', ug, bbr)
    bui = jnp.einsum('btgh,gph->btgp', ug, bbi)
    ar = jnp.broadcast_to(abr, bur.shape)
    ai = jnp.broadcast_to(abi, bui.shape)
    acr, aci, xr, xi = lax.associative_scan(_cplx_combine, (ar, ai, bur, bui), axis=1)
    h0r = h0[..., 0].astype(F32)[:, None]
    h0i = h0[..., 1].astype(F32)[:, None]
    xr, xi = xr + acr * h0r - aci * h0i, xi + acr * h0i + aci * h0r
    y = (jnp.einsum('btgp,ghp->btgh', xr, c_re.astype(F32))
         - jnp.einsum('btgp,ghp->btgh', xi, c_im.astype(F32))
         + d_skip.astype(F32) * ug)
    y = jax.nn.gelu(y.reshape(bsz, T, S5_WIDTH))
    y = y * jax.nn.sigmoid(y @ w_glu.astype(F32))
    return y.astype(u.dtype), jnp.stack([xr[:, -1], xi[:, -1]], axis=-1)


def fox_attend(q, k, v, cum_q, cum_k, q_pos, k_pos):
    s = jnp.einsum('bqhd,bkhd->bhqk', q, k, preferred_element_type=F32) * FOX_SCALE
    s = s + jnp.transpose(cum_q, (0, 2, 1))[..., :, None] - jnp.transpose(cum_k, (0, 2, 1))[:, :, None, :]
    s = jnp.where(k_pos[None, :] <= q_pos[:, None], s, -jnp.inf)
    p = jax.nn.softmax(s, axis=-1)
    return jnp.einsum('bhqk,bkhd->bqhd', p.astype(v.dtype), v)


def fox_prompt(q, k, v, logf):
    bsz, T, H, Dh = q.shape
    cum = jnp.cumsum(logf, axis=1)
    k_pos = jnp.arange(T)

    def block(i):
        s0 = i * Q_BLOCK
        qb = lax.dynamic_slice_in_dim(q, s0, Q_BLOCK, axis=1)
        cb = lax.dynamic_slice_in_dim(cum, s0, Q_BLOCK, axis=1)
        return fox_attend(qb, k, v, cb, cum, s0 + jnp.arange(Q_BLOCK), k_pos)

    o = lax.map(block, jnp.arange(T // Q_BLOCK))
    return jnp.transpose(o, (1, 0, 2, 3, 4)).reshape(bsz, T, H, Dh)


def fox_sample(q, k, v, logf, k_past, v_past, logf_past):
    Lp, T = k_past.shape[1], q.shape[1]
    k_all = jnp.concatenate([k_past.astype(k.dtype), k], axis=1)
    v_all = jnp.concatenate([v_past.astype(v.dtype), v], axis=1)
    cum = jnp.cumsum(jnp.concatenate([logf_past.astype(F32), logf], axis=1), axis=1)
    return fox_attend(q, k_all, v_all, cum[:, Lp:], cum, Lp + jnp.arange(T), jnp.arange(Lp + T))


def ssd_scan(x, a, bm, cm, h0):
    bsz, T, H, P = x.shape
    chunk = SSD_CHUNK if T % SSD_CHUNK == 0 else T
    nc = T // chunk
    rep = H // bm.shape[2]
    bh = jnp.repeat(bm, rep, axis=2).reshape(bsz, nc, chunk, H, -1)
    ch = jnp.repeat(cm, rep, axis=2).reshape(bsz, nc, chunk, H, -1)
    xc = x.reshape(bsz, nc, chunk, H, P)
    acum = jnp.cumsum(jnp.transpose(a.reshape(bsz, nc, chunk, H), (0, 3, 1, 2)), axis=-1)
    li = jnp.arange(chunk)
    lmat = jnp.exp(jnp.where(li[:, None] >= li[None, :],
                             acum[..., :, None] - acum[..., None, :], -jnp.inf))
    scores = jnp.einsum('bclhn,bcshn->bhcls', ch, bh) * lmat
    y_diag = jnp.einsum('bhcls,bcshp->bclhp', scores, xc)
    decay_states = jnp.exp(acum[..., -1:] - acum)
    states = jnp.einsum('bclhn,bhcl,bclhp->bchpn', bh, decay_states, xc)
    states = jnp.concatenate([h0[:, None], states], axis=1)
    cs = jnp.cumsum(jnp.pad(acum[..., -1], ((0, 0), (0, 0), (1, 0))), axis=-1)
    ci = jnp.arange(nc + 1)
    decay_chunk = jnp.exp(jnp.where(ci[:, None] >= ci[None, :],
                                    cs[..., :, None] - cs[..., None, :], -jnp.inf))
    new_states = jnp.einsum('bhzc,bchpn->bzhpn', decay_chunk, states)
    y_off = jnp.einsum('bclhn,bchpn,bhcl->bclhp', ch, new_states[:, :-1], jnp.exp(acum))
    return (y_diag + y_off).reshape(bsz, T, H, P), new_states[:, -1]


def ssd_mix(z, xbc_raw, dt_raw, conv_buf, h0, conv_w, conv_b, dt_bias, a_log, d_skip, norm_g):
    xbc, new_buf = causal_dwconv(xbc_raw, conv_buf, conv_w)
    xbc = jax.nn.silu(xbc + conv_b).astype(F32)
    bsz, T, _ = xbc.shape
    nb = SSD_GROUPS * SSD_STATE
    xs = xbc[..., :SSD_WIDTH].reshape(bsz, T, SSD_HEADS, SSD_HEAD_DIM)
    bm = xbc[..., SSD_WIDTH:SSD_WIDTH + nb].reshape(bsz, T, SSD_GROUPS, SSD_STATE)
    cm = xbc[..., SSD_WIDTH + nb:].reshape(bsz, T, SSD_GROUPS, SSD_STATE)
    dt = jax.nn.softplus(dt_raw.astype(F32) + dt_bias.astype(F32))
    a = -jnp.exp(a_log.astype(F32))
    y, h_new = ssd_scan(xs * dt[..., None], dt * a, bm, cm, h0.astype(F32))
    y = y + d_skip.astype(F32)[:, None] * xs
    y = y.reshape(bsz, T, SSD_WIDTH).astype(z.dtype) * jax.nn.silu(z)
    return rmsnorm(y, norm_g), h_new, new_buf


def short_conv_mix(bg, cg, hh, buf, w):
    yc, new_buf = causal_dwconv(cg * hh, buf, w)
    return bg * yc, new_buf


def hybrid_layer(x, P, l, h_s5, h_ssd, buf_ssd, buf_sc, past):
    bsz, T, _ = x.shape
    xn = rmsnorm(x, P['norm_mix_g'][l])
    u, q, k, v, fr, z, xbc, dtr, sb, sc, sh = split_cols(xn @ P['w_in'][l])
    ya, h_s5n = s5_mix(u, h_s5, P['s5_lam_re'][l], P['s5_lam_im'][l], P['s5_log_dt'][l],
                       P['s5_b_re'][l], P['s5_b_im'][l], P['s5_c_re'][l], P['s5_c_im'][l],
                       P['s5_d'][l], P['s5_w_glu'][l])
    ya = rmsnorm(ya, P['s5_norm_g'][l])
    q = q.reshape(bsz, T, FOX_HEADS, FOX_HEAD_DIM)
    k = k.reshape(bsz, T, FOX_HEADS, FOX_HEAD_DIM)
    v = v.reshape(bsz, T, FOX_HEADS, FOX_HEAD_DIM)
    logf = jax.nn.log_sigmoid(fr.astype(F32) + P['fox_b_f'][l].astype(F32))
    if past is None:
        yb = fox_prompt(q, k, v, logf)
    else:
        yb = fox_sample(q, k, v, logf, past[0], past[1], past[2])
    yb = rmsnorm(yb.reshape(bsz, T, FOX_WIDTH), P['fox_norm_g'][l])
    yc, h_ssdn, buf_ssdn = ssd_mix(z, xbc, dtr, buf_ssd, h_ssd, P['ssd_conv_w'][l], P['ssd_conv_b'][l],
                                   P['ssd_dt_bias'][l], P['ssd_a_log'][l], P['ssd_d'][l],
                                   P['ssd_norm_g'][l])
    yd, buf_scn = short_conv_mix(sb, sc, sh, buf_sc, P['sc_conv_w'][l])
    yd = rmsnorm(yd, P['sc_norm_g'][l])
    x = x + jnp.concatenate([ya, yb, yc, yd], axis=-1) @ P['w_out'][l]
    hn = rmsnorm(x, P['norm_ffn_g'][l])
    x = x + jnp.square(jax.nn.relu(hn @ P['w_up'][l])) @ P['w_down'][l]
    return x, (k, v, logf, h_s5n, h_ssdn, buf_ssdn, buf_scn)


def gather_pages(cache, l, page_table):
    g = cache[l, page_table]
    return g.reshape((g.shape[0], g.shape[1] * g.shape[2]) + g.shape[3:])


def stack_layers(states):
    return tuple(jnp.stack(col, axis=0) for col in zip(*states))


def setup_inputs(seed: int = 0) -> dict:
    key = jax.random.key(seed)
    ks = iter(jax.random.split(key, 48))

    def nrm(shape, scale):
        return jax.random.normal(next(ks), shape, F32) * scale

    def unif(shape, lo, hi):
        return jax.random.uniform(next(ks), shape, F32, lo, hi)

    def gain(shape):
        return 1.0 + nrm(shape, 0.02)

    n_pages = PAST_LEN // PAGE_SIZE
    n_pool = (DEC_BATCH * n_pages * POOL_NUM) // POOL_DEN
    x_prompt = nrm((BATCH, SEQ, D_MODEL), 1.0)
    x_sample = nrm((DEC_BATCH, DEC_SEQ, D_MODEL), 1.0)
    cache_k = nrm((DEPTH, n_pool, PAGE_SIZE, FOX_HEADS, FOX_HEAD_DIM), 1.0)
    cache_v = nrm((DEPTH, n_pool, PAGE_SIZE, FOX_HEADS, FOX_HEAD_DIM), 1.0)
    cache_logf = jax.nn.log_sigmoid(nrm((DEPTH, n_pool, PAGE_SIZE, FOX_HEADS), 1.0) + 4.0)
    state_s5 = nrm((DEPTH, DEC_BATCH, S5_GROUPS, S5_STATE, 2), 0.1)
    state_ssd = nrm((DEPTH, DEC_BATCH, SSD_HEADS, SSD_HEAD_DIM, SSD_STATE), 0.1)
    state_ssd_conv = nrm((DEPTH, DEC_BATCH, SSD_CONV - 1, SSD_XBC), 1.0)
    state_sconv = nrm((DEPTH, DEC_BATCH, SC_CONV - 1, SC_WIDTH), 1.0)
    page_table = jax.random.permutation(next(ks), n_pool)[:DEC_BATCH * n_pages].reshape(
        DEC_BATCH, n_pages).astype(jnp.int32)
    norm_mix_g = gain((DEPTH, D_MODEL))
    w_in = nrm((DEPTH, D_MODEL, IN_COLS), D_MODEL ** -0.5)
    s5_lam_re = -0.5 + nrm((DEPTH, S5_GROUPS, S5_STATE), 0.01)
    s5_lam_im = math.pi * jnp.arange(S5_STATE, dtype=F32) + nrm((DEPTH, S5_GROUPS, S5_STATE), 0.01)
    s5_log_dt = unif((DEPTH, S5_GROUPS), math.log(1e-3), math.log(1e-1))
    s5_b_re = nrm((DEPTH, S5_GROUPS, S5_STATE, S5_GROUP), (2 * S5_GROUP) ** -0.5)
    s5_b_im = nrm((DEPTH, S5_GROUPS, S5_STATE, S5_GROUP), (2 * S5_GROUP) ** -0.5)
    s5_c_re = nrm((DEPTH, S5_GROUPS, S5_GROUP, S5_STATE), S5_STATE ** -0.5)
    s5_c_im = nrm((DEPTH, S5_GROUPS, S5_GROUP, S5_STATE), S5_STATE ** -0.5)
    s5_d = nrm((DEPTH, S5_GROUPS, S5_GROUP), 1.0)
    s5_w_glu = nrm((DEPTH, S5_WIDTH, S5_WIDTH), S5_WIDTH ** -0.5)
    s5_norm_g = gain((DEPTH, S5_WIDTH))
    fox_b_f = 3.0 + nrm((DEPTH, FOX_HEADS), 0.5)
    fox_norm_g = gain((DEPTH, FOX_WIDTH))
    ssd_conv_w = nrm((DEPTH, SSD_CONV, SSD_XBC), SSD_CONV ** -0.5)
    ssd_conv_b = nrm((DEPTH, SSD_XBC), 0.02)
    dt0 = jnp.exp(unif((DEPTH, SSD_HEADS), math.log(1e-3), math.log(1e-1)))
    ssd_dt_bias = dt0 + jnp.log(-jnp.expm1(-dt0))
    ssd_a_log = jnp.log(unif((DEPTH, SSD_HEADS), 1.0, 16.0))
    ssd_d = 1.0 + nrm((DEPTH, SSD_HEADS), 0.1)
    ssd_norm_g = gain((DEPTH, SSD_WIDTH))
    sc_conv_w = nrm((DEPTH, SC_CONV, SC_WIDTH), SC_CONV ** -0.5)
    sc_norm_g = gain((DEPTH, SC_WIDTH))
    w_out = nrm((DEPTH, MIX_WIDTH, D_MODEL), MIX_WIDTH ** -0.5)
    norm_ffn_g = gain((DEPTH, D_MODEL))
    w_up = nrm((DEPTH, D_MODEL, FFN_WIDTH), D_MODEL ** -0.5)
    w_down = nrm((DEPTH, FFN_WIDTH, D_MODEL), FFN_WIDTH ** -0.5)
    norm_final_g = gain((D_MODEL,))
    return {'x_prompt': x_prompt, 'x_sample': x_sample, 'cache_k': cache_k, 'cache_v': cache_v,
            'cache_logf': cache_logf, 'state_s5': state_s5, 'state_ssd': state_ssd,
            'state_ssd_conv': state_ssd_conv, 'state_sconv': state_sconv, 'page_table': page_table,
            'norm_mix_g': norm_mix_g, 'w_in': w_in, 's5_lam_re': s5_lam_re, 's5_lam_im': s5_lam_im,
            's5_log_dt': s5_log_dt, 's5_b_re': s5_b_re, 's5_b_im': s5_b_im, 's5_c_re': s5_c_re,
            's5_c_im': s5_c_im, 's5_d': s5_d, 's5_w_glu': s5_w_glu, 's5_norm_g': s5_norm_g,
            'fox_b_f': fox_b_f, 'fox_norm_g': fox_norm_g, 'ssd_conv_w': ssd_conv_w,
            'ssd_conv_b': ssd_conv_b, 'ssd_dt_bias': ssd_dt_bias, 'ssd_a_log': ssd_a_log,
            'ssd_d': ssd_d, 'ssd_norm_g': ssd_norm_g, 'sc_conv_w': sc_conv_w, 'sc_norm_g': sc_norm_g,
            'w_out': w_out, 'norm_ffn_g': norm_ffn_g, 'w_up': w_up, 'w_down': w_down,
            'norm_final_g': norm_final_g}


def reference(x_prompt, x_sample, cache_k, cache_v, cache_logf, state_s5, state_ssd,
              state_ssd_conv, state_sconv, page_table, norm_mix_g, w_in, s5_lam_re, s5_lam_im,
              s5_log_dt, s5_b_re, s5_b_im, s5_c_re, s5_c_im, s5_d, s5_w_glu, s5_norm_g,
              fox_b_f, fox_norm_g, ssd_conv_w, ssd_conv_b, ssd_dt_bias, ssd_a_log, ssd_d,
              ssd_norm_g, sc_conv_w, sc_norm_g, w_out, norm_ffn_g, w_up, w_down, norm_final_g):
    P = dict(norm_mix_g=norm_mix_g, w_in=w_in, s5_lam_re=s5_lam_re, s5_lam_im=s5_lam_im,
             s5_log_dt=s5_log_dt, s5_b_re=s5_b_re, s5_b_im=s5_b_im, s5_c_re=s5_c_re,
             s5_c_im=s5_c_im, s5_d=s5_d, s5_w_glu=s5_w_glu, s5_norm_g=s5_norm_g,
             fox_b_f=fox_b_f, fox_norm_g=fox_norm_g, ssd_conv_w=ssd_conv_w, ssd_conv_b=ssd_conv_b,
             ssd_dt_bias=ssd_dt_bias, ssd_a_log=ssd_a_log, ssd_d=ssd_d, ssd_norm_g=ssd_norm_g,
             sc_conv_w=sc_conv_w, sc_norm_g=sc_norm_g, w_out=w_out, norm_ffn_g=norm_ffn_g,
             w_up=w_up, w_down=w_down)

    bp = x_prompt.shape[0]
    h = x_prompt
    prompt_states = []
    for l in range(DEPTH):
        h, st = hybrid_layer(
            h, P, l,
            jnp.zeros((bp, S5_GROUPS, S5_STATE, 2), F32),
            jnp.zeros((bp, SSD_HEADS, SSD_HEAD_DIM, SSD_STATE), F32),
            jnp.zeros((bp, SSD_CONV - 1, SSD_XBC), x_prompt.dtype),
            jnp.zeros((bp, SC_CONV - 1, SC_WIDTH), x_prompt.dtype),
            None)
        prompt_states.append(st)
    y_prompt = rmsnorm(h, norm_final_g)
    (k_prompt, v_prompt, logf_prompt, s5_prompt, ssd_prompt,
     ssd_conv_prompt, sconv_prompt) = stack_layers(prompt_states)

    h = x_sample
    sample_states = []
    for l in range(DEPTH):
        past = (gather_pages(cache_k, l, page_table),
                gather_pages(cache_v, l, page_table),
                gather_pages(cache_logf, l, page_table))
        h, st = hybrid_layer(h, P, l, state_s5[l], state_ssd[l], state_ssd_conv[l],
                             state_sconv[l], past)
        sample_states.append(st)
    y_sample = rmsnorm(h, norm_final_g)
    (k_sample, v_sample, logf_sample, s5_sample, ssd_sample,
     ssd_conv_sample, sconv_sample) = stack_layers(sample_states)

    return (y_prompt, y_sample,
            k_prompt, v_prompt, logf_prompt, s5_prompt, ssd_prompt, ssd_conv_prompt, sconv_prompt,
            k_sample, v_sample, logf_sample, s5_sample, ssd_sample, ssd_conv_sample, sconv_sample)
```

```python
import functools
import math

import jax
import jax.numpy as jnp
from jax import lax
from jax.experimental import pallas as pl
from jax.experimental.pallas import tpu as pltpu

F32 = jnp.float32
BF16 = jnp.bfloat16

D_MODEL = 1024
DEPTH = 4
PAGE = 128
HEADS = 4
HEAD_DIM = 64
GROUP_W = 256
S5_G, S5_H, S5_P = 16, 16, 64
S5_CH = S5_G * S5_P
SSD_STATE = 128
SSD_CONV = 4
SC_CONV = 3
FFN = 4096
EPS = 1e-5
FOX_SCALE = HEAD_DIM ** -0.5
NEG = -1e30

COL_U, COL_Q, COL_K, COL_V, COL_Z, COL_XS, COL_B, COL_C, COL_SB, COL_SC, COL_SH = range(11)
MISC_OFF = 11 * GROUP_W
PROJ_W = MISC_OFF + 128
MISC_BLK = MISC_OFF // 128
DT_LANE = 4

VMEM_LIMIT = 56 * 1024 * 1024


def _cparams(sem):
    return pltpu.CompilerParams(dimension_semantics=sem, vmem_limit_bytes=VMEM_LIMIT)


def _const_spec(shape):
    nd = len(shape)
    return pl.BlockSpec(shape, lambda *_: (0,) * nd, pipeline_mode=pl.Buffered(1))


def _split3(x):
    hi = x.astype(BF16)
    r1 = x - hi.astype(F32)
    mid = r1.astype(BF16)
    lo = (r1 - mid.astype(F32)).astype(BF16)
    return hi, mid, lo


def _dot(a, b):
    return jnp.dot(a, b, preferred_element_type=F32)


def _dot_nt(a, b):
    return lax.dot_general(a, b, (((1,), (1,)), ((), ())), preferred_element_type=F32)


def _dot3_l(m_bf16, x):
    hi, mid, lo = _split3(x)
    return _dot(m_bf16, hi) + _dot(m_bf16, mid) + _dot(m_bf16, lo)


def _dot3_r(x, m_bf16):
    hi, mid, lo = _split3(x)
    return _dot(hi, m_bf16) + _dot(mid, m_bf16) + _dot(lo, m_bf16)


def _iota_mask(shape, fn):
    r = lax.broadcasted_iota(jnp.int32, shape, 0)
    c = lax.broadcasted_iota(jnp.int32, shape, 1)
    return fn(r, c)


def _tri_bf16(shape, fn):
    return jnp.where(_iota_mask(shape, fn), 1.0, 0.0).astype(BF16)


def _rms(x, g):
    return x * lax.rsqrt(jnp.mean(x * x, axis=-1, keepdims=True) + EPS) * g


def _sigmoid(x):
    return 1.0 / (1.0 + jnp.exp(-x))


def _softplus(x):
    return jnp.maximum(x, 0.0) + jnp.log1p(jnp.exp(-jnp.abs(x)))


def _lane_expand(cols, lane_head):
    out = jnp.where(lane_head == 2, cols[2], cols[3])
    out = jnp.where(lane_head == 1, cols[1], out)
    return jnp.where(lane_head == 0, cols[0], out)


def _in_proj_kernel(x_ref, g_ref, w_ref, o_ref):
    xn = _rms(x_ref[...], g_ref[...])
    o_ref[...] = _dot(xn.astype(BF16), w_ref[...])


def _in_proj(x, g, w, tm):
    n = x.shape[0]
    return pl.pallas_call(
        _in_proj_kernel,
        grid=(n // tm,),
        in_specs=[pl.BlockSpec((tm, D_MODEL), lambda i: (i, 0)),
                  _const_spec((1, D_MODEL)),
                  _const_spec((D_MODEL, PROJ_W))],
        out_specs=pl.BlockSpec((tm, PROJ_W), lambda i: (i, 0)),
        out_shape=jax.ShapeDtypeStruct((n, PROJ_W), F32),
        compiler_params=_cparams(("parallel",)),
        name="in_proj",
    )(x, g, w)


def _logf_cum_kernel(m_ref, b_ref, logf_ref, cum_ref, carry_ref, *, lc):
    @pl.when(pl.program_id(1) == 0)
    def _():
        carry_ref[...] = jnp.zeros_like(carry_ref)

    x = m_ref[...] + b_ref[...]
    logf = jnp.minimum(x, 0.0) - jnp.log1p(jnp.exp(-jnp.abs(x)))
    tri = _tri_bf16((lc, lc), lambda r, c: c <= r)
    cum = _dot3_l(tri, logf) + carry_ref[...]
    logf_ref[...] = logf
    cum_ref[...] = cum
    carry_ref[...] = cum[lc - 1:lc, :]


def _logf_cum(proj, bias, nseq, t, lc):
    nc = t // lc
    n = nseq * t
    return pl.pallas_call(
        functools.partial(_logf_cum_kernel, lc=lc),
        grid=(nseq, nc),
        in_specs=[pl.BlockSpec((lc, 128), lambda b, c: (b * nc + c, MISC_BLK)),
                  _const_spec((1, 128))],
        out_specs=[pl.BlockSpec((lc, 128), lambda b, c: (b * nc + c, 0)),
                   pl.BlockSpec((lc, 128), lambda b, c: (b * nc + c, 0))],
        out_shape=[jax.ShapeDtypeStruct((n, 128), F32), jax.ShapeDtypeStruct((n, 128), F32)],
        scratch_shapes=[pltpu.VMEM((1, 128), F32)],
        compiler_params=_cparams(("parallel", "arbitrary")),
        name="logf_cum",
    )(proj, bias)


def _s5_kernel(u_ref, h0_ref, bri_ref, ak_ref, ct_ref, d_ref, wg_ref, g_ref,
               y_ref, st_ref, bu_ref, xc_ref, *, tc, state_row):
    c = pl.program_id(1)

    @pl.when(c == 0)
    def _():
        h0 = h0_ref[...]
        xc_ref[0] = jnp.broadcast_to(h0[:, :S5_CH], (8, S5_CH))
        xc_ref[1] = jnp.broadcast_to(h0[:, S5_CH:], (8, S5_CH))

    u = u_ref[...]
    bu_ref[...] = _dot(u.astype(BF16), bri_ref[...])

    def tile(i, carry):
        xr_c, xi_c = carry
        r0 = pl.multiple_of(i * 8, 8)
        dr = bu_ref[pl.ds(r0, 8), 0:S5_CH]
        di = bu_ref[pl.ds(r0, 8), S5_CH:2 * S5_CH]
        for j, k in enumerate((1, 2, 4)):
            ar = ak_ref[2 + 2 * j]
            ai = ak_ref[3 + 2 * j]
            sr = pltpu.roll(dr, k, axis=0)
            si = pltpu.roll(di, k, axis=0)
            dr, di = dr + ar * sr - ai * si, di + ar * si + ai * sr
        pr = ak_ref[0]
        pi = ak_ref[1]
        xr = dr + pr * xr_c - pi * xi_c
        xi = di + pr * xi_c + pi * xr_c
        bu_ref[pl.ds(r0, 8), 0:S5_CH] = xr
        bu_ref[pl.ds(r0, 8), S5_CH:2 * S5_CH] = xi
        return (jnp.broadcast_to(xr[7:8, :], (8, S5_CH)),
                jnp.broadcast_to(xi[7:8, :], (8, S5_CH)))

    xr_c, xi_c = lax.fori_loop(0, tc // 8, tile, (xc_ref[0], xc_ref[1]))
    xc_ref[0] = xr_c
    xc_ref[1] = xi_c

    @pl.when(c == pl.num_programs(1) - 1)
    def _():
        st_ref[...] = bu_ref[state_row:state_row + 1, :]

    y = _dot(bu_ref[...].astype(BF16), ct_ref[...]) + d_ref[...] * u
    y = 0.5 * y * (1.0 + jnp.tanh(math.sqrt(2.0 / math.pi) * (y + 0.044715 * (y * y * y))))
    y = y * _sigmoid(_dot(y.astype(BF16), wg_ref[...]))
    y_ref[...] = _rms(y, g_ref[...])


def _s5(proj, h0, bri, ak, ct, d, wg, g, nseq, t, tc, state_row):
    nc = t // tc
    n = nseq * t
    return pl.pallas_call(
        functools.partial(_s5_kernel, tc=tc, state_row=state_row),
        grid=(nseq, nc),
        in_specs=[pl.BlockSpec((tc, GROUP_W), lambda b, c: (b * nc + c, COL_U)),
                  pl.BlockSpec((None, 1, 2 * S5_CH), lambda b, c: (b, 0, 0)),
                  _const_spec((GROUP_W, 2 * S5_CH)),
                  _const_spec((8, 8, S5_CH)),
                  _const_spec((2 * S5_CH, GROUP_W)),
                  _const_spec((1, GROUP_W)),
                  _const_spec((GROUP_W, GROUP_W)),
                  _const_spec((1, GROUP_W))],
        out_specs=[pl.BlockSpec((tc, GROUP_W), lambda b, c: (b * nc + c, 0)),
                   pl.BlockSpec((None, 1, 2 * S5_CH), lambda b, c: (b, 0, 0))],
        out_shape=[jax.ShapeDtypeStruct((n, GROUP_W), F32),
                   jax.ShapeDtypeStruct((nseq, 1, 2 * S5_CH), F32)],
        scratch_shapes=[pltpu.VMEM((tc, 2 * S5_CH), F32), pltpu.VMEM((2, 8, S5_CH), F32)],
        compiler_params=_cparams(("parallel", "arbitrary")),
        name="s5",
    )(proj, h0, bri, ak, ct, d, wg, g)


def _fox_prompt_kernel(q_ref, k_ref, v_ref, cq_ref, ckt_ref, g_ref, o_ref,
                       qm_ref, m_ref, l_ref, acc_ref, *, tq, tk):
    qi = pl.program_id(1)
    ki = pl.program_id(2)
    lane_head = lax.broadcasted_iota(jnp.int32, (1, GROUP_W), 1) // HEAD_DIM

    @pl.when(ki == 0)
    def _():
        q = q_ref[...] * FOX_SCALE
        for h in range(HEADS):
            qm_ref[h] = jnp.where(lane_head == h, q, 0.0).astype(BF16)
        m_ref[...] = jnp.full_like(m_ref, NEG)
        l_ref[...] = jnp.zeros_like(l_ref)
        acc_ref[...] = jnp.zeros_like(acc_ref)

    @pl.when(ki <= qi)
    def _():
        kb = k_ref[...].astype(BF16)
        vb = v_ref[...].astype(BF16)
        cq = cq_ref[...]
        ckt = ckt_ref[...]
        causal = _iota_mask((tq, tk), lambda r, c: (ki * tk + c) <= (qi * tq + r))
        for h in range(HEADS):
            s = _dot_nt(qm_ref[h], kb) + cq[:, h:h + 1] - ckt[h:h + 1, :]
            s = jnp.where(causal, s, NEG)
            m_old = m_ref[h]
            m_new = jnp.maximum(m_old, jnp.max(s, axis=1, keepdims=True))
            alpha = jnp.exp(m_old - m_new)
            p = jnp.exp(s - m_new)
            l_ref[h] = alpha * l_ref[h] + jnp.sum(p, axis=1, keepdims=True)
            acc_ref[h] = alpha * acc_ref[h] + _dot(p.astype(BF16), vb)
            m_ref[h] = m_new

    @pl.when(ki == qi)
    def _():
        o = jnp.zeros((tq, GROUP_W), F32)
        for h in range(HEADS):
            o = jnp.where(lane_head == h, acc_ref[h] / l_ref[h], o)
        o_ref[...] = _rms(o, g_ref[...])


def _fox_prompt(proj, cum, cumt, g, nseq, t, tq):
    tk = tq
    nq = t // tq
    n = nseq * t
    return pl.pallas_call(
        functools.partial(_fox_prompt_kernel, tq=tq, tk=tk),
        grid=(nseq, nq, nq),
        in_specs=[pl.BlockSpec((tq, GROUP_W), lambda b, i, j: (b * nq + i, COL_Q)),
                  pl.BlockSpec((tk, GROUP_W), lambda b, i, j: (b * nq + jnp.minimum(i, j), COL_K)),
                  pl.BlockSpec((tk, GROUP_W), lambda b, i, j: (b * nq + jnp.minimum(i, j), COL_V)),
                  pl.BlockSpec((tq, 128), lambda b, i, j: (b * nq + i, 0)),
                  pl.BlockSpec((None, HEADS, tk), lambda b, i, j: (b, 0, jnp.minimum(i, j))),
                  _const_spec((1, GROUP_W))],
        out_specs=pl.BlockSpec((tq, GROUP_W), lambda b, i, j: (b * nq + i, 0)),
        out_shape=jax.ShapeDtypeStruct((n, GROUP_W), F32),
        scratch_shapes=[pltpu.VMEM((HEADS, tq, GROUP_W), BF16),
                        pltpu.VMEM((HEADS, tq, 1), F32),
                        pltpu.VMEM((HEADS, tq, 1), F32),
                        pltpu.VMEM((HEADS, tq, GROUP_W), F32)],
        compiler_params=_cparams(("parallel", "parallel", "arbitrary")),
        name="fox_prompt",
    )(proj, proj, proj, cum, cumt, g)


PAGE_ROWS = PAGE * HEADS


def _past_bias_kernel(pt_ref, pool_ref, o_ref, lbuf_ref, sem, *, layer, n_pages):
    b = pl.program_id(0)

    def page_copy(p):
        return pltpu.make_async_copy(pool_ref.at[layer, pt_ref[b * n_pages + p]],
                                     lbuf_ref.at[pl.ds(p * 8, 8)], sem.at[0])

    def start(p, _):
        page_copy(p).start()
        return 0

    def wait(p, _):
        page_copy(p).wait()
        return 0

    lax.fori_loop(0, n_pages, start, 0)
    lax.fori_loop(0, n_pages, wait, 0)

    x = lbuf_ref[...]
    rows = n_pages * 8
    mt4 = _tri_bf16((PAGE, PAGE_ROWS), lambda r, c: r > c // HEADS)
    local4 = _dot3_r(x, mt4)
    ones = jnp.ones((PAGE, 128), BF16)
    tot = _dot3_r(x, ones)
    later = _tri_bf16((rows, rows), lambda r, c: (c % 8 == r % 8) & (c // 8 > r // 8))
    carry = _dot3_l(later, tot)
    o_ref[...] = local4 + jnp.concatenate([carry] * HEADS, axis=1)


def _past_bias(page_table_flat, pool_t, layer, nseq, n_pages):
    rows = n_pages * 8
    grid_spec = pltpu.PrefetchScalarGridSpec(
        num_scalar_prefetch=1,
        grid=(nseq,),
        in_specs=[pl.BlockSpec(memory_space=pl.ANY)],
        out_specs=pl.BlockSpec((None, rows, PAGE_ROWS), lambda b, pt: (b, 0, 0)),
        scratch_shapes=[pltpu.VMEM((rows, 128), F32), pltpu.SemaphoreType.DMA((1,))],
    )
    return pl.pallas_call(
        functools.partial(_past_bias_kernel, layer=layer, n_pages=n_pages),
        grid_spec=grid_spec,
        out_shape=jax.ShapeDtypeStruct((nseq, rows, PAGE_ROWS), F32),
        compiler_params=_cparams(("arbitrary",)),
        name="past_bias",
    )(page_table_flat, pool_t)


def _fox_sample_kernel(pt_ref, ck_ref, cv_ref, q_ref, kn_ref, vn_ref, r4_ref, ln_ref, mc_ref,
                       o_ref, kbuf, vbuf, sem, m_ref, l_ref, acc_ref,
                       *, layer, n_pages, pg, dec_t):
    g = pl.program_id(0)
    ng = pl.num_programs(0)
    nc = n_pages // pg
    c = g % nc
    slot = g % 2
    rows16 = HEADS * dec_t

    def copies(step, sl):
        out = []
        for j in range(pg):
            pid = pt_ref[step * pg + j]
            out.append(pltpu.make_async_copy(ck_ref.at[layer, pid], kbuf.at[sl, j], sem.at[0, sl]))
            out.append(pltpu.make_async_copy(cv_ref.at[layer, pid], vbuf.at[sl, j], sem.at[1, sl]))
        return out

    @pl.when(g == 0)
    def _():
        for cp in copies(0, 0):
            cp.start()

    @pl.when(g + 1 < ng)
    def _():
        for cp in copies(g + 1, 1 - slot):
            cp.start()

    for cp in copies(g, slot):
        cp.wait()

    row = lax.broadcasted_iota(jnp.int32, (rows16, 1), 0)
    row_head = row // dec_t
    row_tok = row % dec_t
    ln = ln_ref[...]
    cums = [ln[:, 0:1]]
    for t in range(1, dec_t):
        cums.append(cums[-1] + ln[:, t:t + 1])
    pcol = cums[dec_t - 1]
    for t in range(dec_t - 2, -1, -1):
        pcol = jnp.where(row_tok == t, cums[t], pcol)

    @pl.when(c == 0)
    def _():
        m_ref[...] = jnp.full_like(m_ref, NEG)
        l_ref[...] = jnp.zeros_like(l_ref)
        acc_ref[...] = jnp.zeros_like(acc_ref)

    q16 = q_ref[...].astype(BF16)
    mc = mc_ref[...]

    def attend(s, vb):
        m_old = m_ref[...]
        m_new = jnp.maximum(m_old, jnp.max(s, axis=1, keepdims=True))
        alpha = jnp.exp(m_old - m_new)
        p = jnp.exp(s - m_new)
        l_ref[...] = alpha * l_ref[...] + jnp.sum(p, axis=1, keepdims=True)
        acc_ref[...] = alpha * acc_ref[...] + _dot(p.astype(BF16), vb)
        m_ref[...] = m_new

    for j in range(pg):
        kb = kbuf[slot, j].astype(BF16)
        vb = vbuf[slot, j].astype(BF16)
        r8 = r4_ref[j * 8:(j + 1) * 8, :]
        rb = jnp.broadcast_to(r8[HEADS - 1:HEADS, :], (rows16, PAGE_ROWS))
        for h in range(HEADS - 2, -1, -1):
            rb = jnp.where(row_head == h, jnp.broadcast_to(r8[h:h + 1, :], (rows16, PAGE_ROWS)), rb)
        s = _dot_nt(q16, kb) + rb + mc + pcol
        attend(s, vb)

    @pl.when(c == nc - 1)
    def _():
        lane = lax.broadcasted_iota(jnp.int32, (1, 128), 1)
        lane_tok = lane // HEADS
        lane_head = lane % HEADS
        pt_l = jnp.broadcast_to(cums[dec_t - 1], (rows16, 128))
        for t in range(dec_t - 2, -1, -1):
            pt_l = jnp.where(lane_tok == t, cums[t], pt_l)
        ok = (lane < rows16) & (lane_head == row_head) & (lane_tok <= row_tok)
        s = _dot_nt(q16, kn_ref[...].astype(BF16)) + pcol - pt_l
        s = jnp.where(ok, s, NEG)
        attend(s, vn_ref[...].astype(BF16))
        o_ref[...] = acc_ref[...] / l_ref[...]


def _fox_sample(pt_flat, ck2, cv2, q16, kn, vn, r4, ln16, maskc, layer, nseq, n_pages, pg, dec_t):
    nc = n_pages // pg
    rows16 = HEADS * dec_t
    grid_spec = pltpu.PrefetchScalarGridSpec(
        num_scalar_prefetch=1,
        grid=(nseq * nc,),
        in_specs=[pl.BlockSpec(memory_space=pl.ANY),
                  pl.BlockSpec(memory_space=pl.ANY),
                  pl.BlockSpec((None, rows16, HEAD_DIM), lambda g, pt: (g // nc, 0, 0)),
                  pl.BlockSpec((None, 128, HEAD_DIM), lambda g, pt: (g // nc, 0, 0)),
                  pl.BlockSpec((None, 128, HEAD_DIM), lambda g, pt: (g // nc, 0, 0)),
                  pl.BlockSpec((None, pg * 8, PAGE_ROWS), lambda g, pt: (g // nc, g % nc, 0)),
                  pl.BlockSpec((None, rows16, 128), lambda g, pt: (g // nc, 0, 0)),
                  pl.BlockSpec((rows16, PAGE_ROWS), lambda g, pt: (0, 0))],
        out_specs=pl.BlockSpec((None, rows16, HEAD_DIM), lambda g, pt: (g // nc, 0, 0)),
        scratch_shapes=[pltpu.VMEM((2, pg, PAGE_ROWS, HEAD_DIM), F32),
                        pltpu.VMEM((2, pg, PAGE_ROWS, HEAD_DIM), F32),
                        pltpu.SemaphoreType.DMA((2, 2)),
                        pltpu.VMEM((rows16, 1), F32),
                        pltpu.VMEM((rows16, 1), F32),
                        pltpu.VMEM((rows16, HEAD_DIM), F32)],
    )
    return pl.pallas_call(
        functools.partial(_fox_sample_kernel, layer=layer, n_pages=n_pages, pg=pg, dec_t=dec_t),
        grid_spec=grid_spec,
        out_shape=jax.ShapeDtypeStruct((nseq, rows16, HEAD_DIM), F32),
        compiler_params=_cparams(("arbitrary",)),
        name="fox_sample",
    )(pt_flat, ck2, cv2, q16, kn, vn, r4, ln16, maskc)


def _rms_rows_kernel(x_ref, g_ref, o_ref):
    o_ref[...] = _rms(x_ref[...], g_ref[...])


def _rms_rows(x, g):
    return pl.pallas_call(
        _rms_rows_kernel,
        out_shape=jax.ShapeDtypeStruct(x.shape, F32),
        name="rms_rows",
    )(x, g)


def _ssd_kernel(z_ref, xs_ref, b_ref, c_ref, sb_ref, sc_ref, sh_ref, misc_ref,
                h0_ref, cb0_ref, sb0_ref,
                cw_ref, cbias_ref, dtb_ref, alog_ref, dskip_ref, ng_ref, scw_ref, scg_ref,
                yc_ref, yd_ref, hout_ref, cbout_ref, sbout_ref,
                ext_ref, exts_ref, h_ref, *, L, valid):
    c = pl.program_id(1)
    last = pl.num_programs(1) - 1
    XW = 3 * GROUP_W

    @pl.when(c == 0)
    def _():
        h_ref[...] = h0_ref[...]
        ext_ref[8 - (SSD_CONV - 1):8, :] = cb0_ref[...]
        exts_ref[8 - (SC_CONV - 1):8, :] = sb0_ref[...]

    ext_ref[8:8 + L, 0:GROUP_W] = xs_ref[...]
    ext_ref[8:8 + L, GROUP_W:2 * GROUP_W] = b_ref[...]
    ext_ref[8:8 + L, 2 * GROUP_W:XW] = c_ref[...]
    exts_ref[8:8 + L, :] = sc_ref[...] * sh_ref[...]

    cw = cw_ref[...]
    acc = cbias_ref[...] + cw[0:1, :] * ext_ref[5:5 + L, :]
    for j in range(1, SSD_CONV):
        acc = acc + cw[j:j + 1, :] * ext_ref[5 + j:5 + j + L, :]
    xbc = acc * _sigmoid(acc)
    xs = xbc[:, 0:GROUP_W]
    bm = xbc[:, GROUP_W:2 * GROUP_W].astype(BF16)
    cm = xbc[:, 2 * GROUP_W:XW].astype(BF16)

    scw = scw_ref[...]
    conv = scw[0:1, :] * exts_ref[6:6 + L, :]
    for j in range(1, SC_CONV):
        conv = conv + scw[j:j + 1, :] * exts_ref[6 + j:6 + j + L, :]
    yd_ref[...] = _rms(sb_ref[...] * conv, scg_ref[...])

    @pl.when(c == last)
    def _():
        cbout_ref[...] = ext_ref[8 + valid - (SSD_CONV - 1):8 + valid, :]
        sbout_ref[...] = exts_ref[8 + valid - (SC_CONV - 1):8 + valid, :]

    ext_ref[8 - (SSD_CONV - 1):8, :] = ext_ref[8 + L - (SSD_CONV - 1):8 + L, :]
    exts_ref[8 - (SC_CONV - 1):8, :] = exts_ref[8 + L - (SC_CONV - 1):8 + L, :]

    lane128 = lax.broadcasted_iota(jnp.int32, (1, 128), 1)
    rowi = lax.broadcasted_iota(jnp.int32, (L, 1), 0)
    live = (lane128 >= DT_LANE) & (lane128 < DT_LANE + HEADS) & (rowi < valid)
    dt = jnp.where(live, _softplus(misc_ref[...] + dtb_ref[...]), 0.0)
    a = dt * (-jnp.exp(alog_ref[...]))
    incl = _tri_bf16((L, L), lambda r, cc: cc <= r)
    after = _tri_bf16((L, L), lambda r, cc: cc > r)
    acum = _dot3_l(incl, a)
    rev = _dot3_l(after, a)
    lower = _iota_mask((L, L), lambda r, cc: cc <= r)
    strict_f = jnp.where(_iota_mask((L, L), lambda r, cc: r > cc), 1.0, 0.0)

    lane_head = lax.broadcasted_iota(jnp.int32, (1, GROUP_W), 1) // HEAD_DIM
    col = lambda arr, h: arr[:, DT_LANE + h:DT_LANE + h + 1]
    dt_l = _lane_expand([col(dt, h) for h in range(HEADS)], lane_head)
    eac_l = _lane_expand([jnp.exp(col(acum, h)) for h in range(HEADS)], lane_head)
    erev_l = _lane_expand([jnp.exp(col(rev, h)) for h in range(HEADS)], lane_head)
    xdt = xs * dt_l
    xdt_b = xdt.astype(BF16)

    gmat = [_dot_nt(cm[:, g * SSD_STATE:(g + 1) * SSD_STATE], bm[:, g * SSD_STATE:(g + 1) * SSD_STATE])
            for g in range(2)]
    y = jnp.zeros((L, GROUP_W), F32)
    for h in range(HEADS):
        delta = _dot3_l(incl, col(a, h) * strict_f)
        lmat = jnp.where(lower, jnp.exp(delta), 0.0)
        scores = (gmat[h // 2] * lmat).astype(BF16)
        y = jnp.where(lane_head == h, _dot(scores, xdt_b), y)

    h_all = h_ref[...]
    hb = h_all.astype(BF16)
    y0 = _dot_nt(cm[:, 0:SSD_STATE], hb)
    y1 = _dot_nt(cm[:, SSD_STATE:2 * SSD_STATE], hb)
    y = y + jnp.concatenate([y0[:, 0:128], y1[:, 128:256]], axis=1) * eac_l

    wt = jnp.transpose(xdt * erev_l).astype(BF16)
    s0 = _dot(wt, bm[:, 0:SSD_STATE])
    s1 = _dot(wt, bm[:, SSD_STATE:2 * SSD_STATE])
    upd = jnp.concatenate([s0[0:128, :], s1[128:256, :]], axis=0)
    row_head = lax.broadcasted_iota(jnp.int32, (GROUP_W, 1), 0) // HEAD_DIM
    dec = [jnp.exp(acum[L - 1:L, DT_LANE + h:DT_LANE + h + 1]) for h in range(HEADS)]
    scale = jnp.where(row_head == 2, dec[2], dec[3])
    scale = jnp.where(row_head == 1, dec[1], scale)
    scale = jnp.where(row_head == 0, dec[0], scale)
    h_new = scale * h_all + upd
    h_ref[...] = h_new

    @pl.when(c == last)
    def _():
        hout_ref[...] = h_new

    y = y + dskip_ref[...] * xs
    zz = z_ref[...]
    y = y * (zz * _sigmoid(zz))
    yc_ref[...] = _rms(y, ng_ref[...])


def _ssd(proj, h0, cb0, sb0, cw, cbias, dtb, alog, dskip, ng, scw, scg, nseq, t, L, valid):
    nc = t // L
    n = nseq * t
    XW = 3 * GROUP_W

    def col_spec(cb):
        return pl.BlockSpec((L, GROUP_W), lambda b, c: (b * nc + c, cb))

    state_specs = [pl.BlockSpec((None, GROUP_W, SSD_STATE), lambda b, c: (b, 0, 0)),
                   pl.BlockSpec((None, SSD_CONV - 1, XW), lambda b, c: (b, 0, 0)),
                   pl.BlockSpec((None, SC_CONV - 1, GROUP_W), lambda b, c: (b, 0, 0))]
    return pl.pallas_call(
        functools.partial(_ssd_kernel, L=L, valid=valid),
        grid=(nseq, nc),
        in_specs=[col_spec(COL_Z), col_spec(COL_XS), col_spec(COL_B), col_spec(COL_C),
                  col_spec(COL_SB), col_spec(COL_SC), col_spec(COL_SH),
                  pl.BlockSpec((L, 128), lambda b, c: (b * nc + c, MISC_BLK))]
                 + state_specs
                 + [_const_spec((SSD_CONV, XW)), _const_spec((1, XW)), _const_spec((1, 128)),
                    _const_spec((1, 128)), _const_spec((1, GROUP_W)), _const_spec((1, GROUP_W)),
                    _const_spec((SC_CONV, GROUP_W)), _const_spec((1, GROUP_W))],
        out_specs=[pl.BlockSpec((L, GROUP_W), lambda b, c: (b * nc + c, 0)),
                   pl.BlockSpec((L, GROUP_W), lambda b, c: (b * nc + c, 0))] + state_specs,
        out_shape=[jax.ShapeDtypeStruct((n, GROUP_W), F32),
                   jax.ShapeDtypeStruct((n, GROUP_W), F32),
                   jax.ShapeDtypeStruct((nseq, GROUP_W, SSD_STATE), F32),
                   jax.ShapeDtypeStruct((nseq, SSD_CONV - 1, XW), F32),
                   jax.ShapeDtypeStruct((nseq, SC_CONV - 1, GROUP_W), F32)],
        scratch_shapes=[pltpu.VMEM((8 + L, XW), F32), pltpu.VMEM((8 + L, GROUP_W), F32),
                        pltpu.VMEM((GROUP_W, SSD_STATE), F32)],
        compiler_params=_cparams(("parallel", "arbitrary")),
        name="ssd",
    )(proj, proj, proj, proj, proj, proj, proj, proj, h0, cb0, sb0,
      cw, cbias, dtb, alog, dskip, ng, scw, scg)


def _out_ffn_kernel(x_ref, ya_ref, yb_ref, yc_ref, yd_ref, wo_ref, g_ref, wu_ref, wd_ref, gf_ref,
                    *out_refs, final):
    x1 = x_ref[...]
    for i, y_ref in enumerate((ya_ref, yb_ref, yc_ref, yd_ref)):
        x1 = x1 + _dot(y_ref[...].astype(BF16), wo_ref[i * GROUP_W:(i + 1) * GROUP_W, :])
    hn = _rms(x1, g_ref[...]).astype(BF16)
    fc = 1024
    parts = []
    for j in range(FFN // fc):
        hj = jnp.maximum(_dot(hn, wu_ref[:, j * fc:(j + 1) * fc]), 0.0)
        parts.append(_dot((hj * hj).astype(BF16), wd_ref[j * fc:(j + 1) * fc, :]))
    acc = x1 + ((parts[0] + parts[1]) + (parts[2] + parts[3]))
    out_refs[0][...] = acc
    if final:
        out_refs[1][...] = _rms(acc, gf_ref[...])


def _out_ffn(x, ya, yb, yc, yd, wo, g, wu, wd, gf, tm, final):
    n = x.shape[0]
    row = lambda w: pl.BlockSpec((tm, w), lambda i: (i, 0))
    n_out = 2 if final else 1
    return pl.pallas_call(
        functools.partial(_out_ffn_kernel, final=final),
        grid=(n // tm,),
        in_specs=[row(D_MODEL), row(GROUP_W), row(GROUP_W), row(GROUP_W), row(GROUP_W),
                  _const_spec((D_MODEL, D_MODEL)), _const_spec((1, D_MODEL)),
                  _const_spec((D_MODEL, FFN)), _const_spec((FFN, D_MODEL)),
                  _const_spec((1, D_MODEL))],
        out_specs=[row(D_MODEL)] * n_out,
        out_shape=[jax.ShapeDtypeStruct((n, D_MODEL), F32)] * n_out,
        compiler_params=_cparams(("parallel",)),
        name="out_ffn",
    )(x, ya, yb, yc, yd, wo, g, wu, wd, gf)


def _s5_params(lam_re, lam_im, log_dt, b_re, b_im, c_re, c_im):
    dt = jnp.exp(log_dt)[:, None]
    mag = jnp.exp(lam_re * dt)
    abr, abi = mag * jnp.cos(lam_im * dt), mag * jnp.sin(lam_im * dt)
    den = lam_re * lam_re + lam_im * lam_im
    qr = ((abr - 1.0) * lam_re + abi * lam_im) / den
    qi = (abi * lam_re - (abr - 1.0) * lam_im) / den
    bbr = qr[..., None] * b_re - qi[..., None] * b_im
    bbi = qr[..., None] * b_im + qi[..., None] * b_re
    eye = jnp.eye(S5_G, dtype=F32)
    bd = lambda m: jnp.einsum('gph,gk->ghkp', m, eye).reshape(S5_G * S5_H, S5_CH)
    bri = jnp.concatenate([bd(bbr), bd(bbi)], axis=1).astype(BF16)
    cd = lambda m: jnp.einsum('ghp,gk->gpkh', m, eye).reshape(S5_CH, S5_G * S5_H)
    ct = jnp.concatenate([cd(c_re), -cd(c_im)], axis=0).astype(BF16)
    ar, ai = abr.reshape(1, S5_CH), abi.reshape(1, S5_CH)
    pows = [(ar, ai)]
    for _ in range(7):
        pr, pi = pows[-1]
        pows.append((pr * ar - pi * ai, pr * ai + pi * ar))
    rows = jnp.arange(8)[:, None]
    tabs = [jnp.concatenate([p[0] for p in pows], axis=0), jnp.concatenate([p[1] for p in pows], axis=0)]
    for k in (1, 2, 4):
        for part in pows[k - 1]:
            tabs.append(jnp.where(rows >= k, part, 0.0))
    return bri, jnp.stack(tabs, axis=0), ct


def _pad_lanes(v, off):
    return jnp.zeros((1, 128), F32).at[0, off:off + v.shape[0]].set(v)


def _permute_w_in(w):
    return jnp.concatenate(
        [w[:, 0:1024], w[:, 1028:2052], w[:, 2056:2824], w[:, 1024:1028], w[:, 2052:2056],
         jnp.zeros((w.shape[0], 120), w.dtype)], axis=1).astype(BF16)


def kernel(x_prompt, x_sample, cache_k, cache_v, cache_logf, state_s5, state_ssd, state_ssd_conv, state_sconv, page_table, norm_mix_g, w_in, s5_lam_re, s5_lam_im, s5_log_dt, s5_b_re, s5_b_im, s5_c_re, s5_c_im, s5_d, s5_w_glu, s5_norm_g, fox_b_f, fox_norm_g, ssd_conv_w, ssd_conv_b, ssd_dt_bias, ssd_a_log, ssd_d, ssd_norm_g, sc_conv_w, sc_norm_g, w_out, norm_ffn_g, w_up, w_down, norm_final_g):
    bp, tp, _ = x_prompt.shape
    bs, ts, _ = x_sample.shape
    n_pool = cache_k.shape[1]
    n_pages = page_table.shape[1]
    np_tok, ns_tok = bp * tp, bs * ts
    rows16 = HEADS * ts
    pad_t = 128

    pt_flat = page_table.reshape(-1)
    ck2 = cache_k.reshape(DEPTH, n_pool, PAGE_ROWS, HEAD_DIM)
    cv2 = cache_v.reshape(DEPTH, n_pool, PAGE_ROWS, HEAD_DIM)
    pool_t = jnp.pad(jnp.swapaxes(cache_logf, 2, 3), ((0, 0), (0, 0), (0, 8 - HEADS), (0, 0)))
    lane = jnp.arange(PAGE_ROWS)[None, :] % HEADS
    rowh = jnp.arange(rows16)[:, None] // ts
    maskc = jnp.where(lane == rowh, 0.0, NEG).astype(F32)
    gfin = norm_final_g.reshape(1, D_MODEL)

    hp = x_prompt.reshape(np_tok, D_MODEL)
    hs = x_sample.reshape(ns_tok, D_MODEL)
    zeros_p = dict(
        s5=jnp.zeros((bp, 1, 2 * S5_CH), F32), ssd=jnp.zeros((bp, GROUP_W, SSD_STATE), F32),
        cb=jnp.zeros((bp, SSD_CONV - 1, 3 * GROUP_W), F32), sb=jnp.zeros((bp, SC_CONV - 1, GROUP_W), F32))
    outs_p, outs_s = [], []
    y_prompt = y_sample = None

    for l in range(DEPTH):
        final = l == DEPTH - 1
        w_in_l = _permute_w_in(w_in[l])
        g_mix = norm_mix_g[l].reshape(1, D_MODEL)
        bri, ak, ct = _s5_params(s5_lam_re[l], s5_lam_im[l], s5_log_dt[l], s5_b_re[l], s5_b_im[l],
                                 s5_c_re[l], s5_c_im[l])
        s5_args = (bri, ak, ct, s5_d[l].reshape(1, GROUP_W), s5_w_glu[l].astype(BF16),
                   s5_norm_g[l].reshape(1, GROUP_W))
        fb = _pad_lanes(fox_b_f[l], 0)
        fg = fox_norm_g[l].reshape(1, GROUP_W)
        ssd_args = (ssd_conv_w[l], ssd_conv_b[l].reshape(1, -1), _pad_lanes(ssd_dt_bias[l], DT_LANE),
                    _pad_lanes(ssd_a_log[l], DT_LANE), jnp.repeat(ssd_d[l], HEAD_DIM).reshape(1, GROUP_W),
                    ssd_norm_g[l].reshape(1, GROUP_W), sc_conv_w[l], sc_norm_g[l].reshape(1, GROUP_W))
        ffn_args = (w_out[l].astype(BF16), norm_ffn_g[l].reshape(1, D_MODEL), w_up[l].astype(BF16),
                    w_down[l].astype(BF16), gfin)

        proj = _in_proj(hp, g_mix, w_in_l, 512)
        logf, cum = _logf_cum(proj, fb, bp, tp, 512)
        cumt = jnp.swapaxes(cum[:, :HEADS].reshape(bp, tp, HEADS), 1, 2)
        ya, st5 = _s5(proj, zeros_p['s5'], *s5_args, bp, tp, 512, 511)
        yb = _fox_prompt(proj, cum, cumt, fg, bp, tp, 512)
        yc, yd, hssd, cbuf, sbuf = _ssd(proj, zeros_p['ssd'], zeros_p['cb'], zeros_p['sb'], *ssd_args,
                                        bp, tp, 256, 256)
        res = _out_ffn(hp, ya, yb, yc, yd, *ffn_args, 512, final)
        hp = res[0]
        if final:
            y_prompt = res[1]
        outs_p.append((
            proj[:, COL_K * GROUP_W:(COL_K + 1) * GROUP_W].reshape(bp, tp, HEADS, HEAD_DIM),
            proj[:, COL_V * GROUP_W:(COL_V + 1) * GROUP_W].reshape(bp, tp, HEADS, HEAD_DIM),
            logf[:, :HEADS].reshape(bp, tp, HEADS),
            jnp.stack([st5[:, 0, :S5_CH].reshape(bp, S5_G, S5_P),
                       st5[:, 0, S5_CH:].reshape(bp, S5_G, S5_P)], axis=-1),
            hssd.reshape(bp, HEADS, HEAD_DIM, SSD_STATE), cbuf, sbuf))

        projs = _in_proj(hs, g_mix, w_in_l, ns_tok)
        logfs, _ = _logf_cum(projs, fb, 1, ns_tok, ns_tok)
        proj_pad = jnp.pad(projs.reshape(bs, ts, PROJ_W), ((0, 0), (0, pad_t - ts), (0, 0)))
        proj_pad = proj_pad.reshape(bs * pad_t, PROJ_W)
        h0s5 = jnp.concatenate([state_s5[l][..., 0].reshape(bs, 1, S5_CH),
                                state_s5[l][..., 1].reshape(bs, 1, S5_CH)], axis=-1)
        ya, st5 = _s5(proj_pad, h0s5, *s5_args, bs, pad_t, pad_t, ts - 1)
        yc, yd, hssd, cbuf, sbuf = _ssd(proj_pad, state_ssd[l].reshape(bs, GROUP_W, SSD_STATE),
                                        state_ssd_conv[l], state_sconv[l], *ssd_args,
                                        bs, pad_t, pad_t, ts)
        unpad = lambda a: a.reshape(bs, pad_t, GROUP_W)[:, :ts].reshape(ns_tok, GROUP_W)

        qs = projs[:, COL_Q * GROUP_W:(COL_Q + 1) * GROUP_W].reshape(bs, ts, HEADS, HEAD_DIM)
        ks = projs[:, COL_K * GROUP_W:(COL_K + 1) * GROUP_W].reshape(bs, ts, HEADS, HEAD_DIM)
        vs = projs[:, COL_V * GROUP_W:(COL_V + 1) * GROUP_W].reshape(bs, ts, HEADS, HEAD_DIM)
        lfs = logfs[:, :HEADS].reshape(bs, ts, HEADS)
        q16 = (jnp.swapaxes(qs, 1, 2) * FOX_SCALE).reshape(bs, rows16, HEAD_DIM)
        padrows = lambda a: jnp.pad(a.reshape(bs, rows16, HEAD_DIM), ((0, 0), (0, 128 - rows16), (0, 0)))
        ln16 = jnp.repeat(jnp.swapaxes(lfs, 1, 2), ts, axis=1)
        ln16 = jnp.pad(ln16, ((0, 0), (0, 0), (0, 128 - ts)))
        r4 = _past_bias(pt_flat, pool_t, l, bs, n_pages)
        o16 = _fox_sample(pt_flat, ck2, cv2, q16, padrows(ks), padrows(vs), r4, ln16, maskc,
                          l, bs, n_pages, 8, ts)
        yb = _rms_rows(jnp.swapaxes(o16.reshape(bs, HEADS, ts, HEAD_DIM), 1, 2).reshape(ns_tok, GROUP_W), fg)
        res = _out_ffn(hs, unpad(ya), yb, unpad(yc), unpad(yd), *ffn_args, ns_tok, final)
        hs = res[0]
        if final:
            y_sample = res[1]
        outs_s.append((
            ks, vs, lfs,
            jnp.stack([st5[:, 0, :S5_CH].reshape(bs, S5_G, S5_P),
                       st5[:, 0, S5_CH:].reshape(bs, S5_G, S5_P)], axis=-1),
            hssd.reshape(bs, HEADS, HEAD_DIM, SSD_STATE), cbuf, sbuf))

    stack = lambda outs: tuple(jnp.stack(col, axis=0) for col in zip(*outs))
    return ((y_prompt.reshape(bp, tp, D_MODEL), y_sample.reshape(bs, ts, D_MODEL))
            + stack(outs_p) + stack(outs_s))
```

```python
import functools
import math

import jax
import jax.numpy as jnp
from jax import lax
from jax.experimental import pallas as pl
from jax.experimental.pallas import tpu as pltpu

F32 = jnp.float32
BF16 = jnp.bfloat16

D_MODEL = 1024
DEPTH = 4
PAGE = 128
HEADS = 4
HEAD_DIM = 64
GROUP_W = 256
S5_G, S5_H, S5_P = 16, 16, 64
S5_CH = S5_G * S5_P
SSD_STATE = 128
SSD_CONV = 4
SC_CONV = 3
FFN = 4096
EPS = 1e-5
FOX_SCALE = HEAD_DIM ** -0.5
NEG = -1e30

COL_U, COL_Q, COL_K, COL_V, COL_Z, COL_XS, COL_B, COL_C, COL_SB, COL_SC, COL_SH = range(11)
MISC_OFF = 11 * GROUP_W
PROJ_W = MISC_OFF + 128
MISC_BLK = MISC_OFF // 128
DT_LANE = 4

VMEM_LIMIT = 56 * 1024 * 1024


def _cparams(sem):
    return pltpu.CompilerParams(dimension_semantics=sem, vmem_limit_bytes=VMEM_LIMIT)


def _const_spec(shape):
    nd = len(shape)
    return pl.BlockSpec(shape, lambda *_: (0,) * nd, pipeline_mode=pl.Buffered(1))


def _split3(x):
    hi = x.astype(BF16)
    r1 = x - hi.astype(F32)
    mid = r1.astype(BF16)
    lo = (r1 - mid.astype(F32)).astype(BF16)
    return hi, mid, lo


def _dot(a, b):
    return jnp.dot(a, b, preferred_element_type=F32)


def _dot_nt(a, b):
    return lax.dot_general(a, b, (((1,), (1,)), ((), ())), preferred_element_type=F32)


def _dot3_l(m_bf16, x):
    hi, mid, lo = _split3(x)
    return _dot(m_bf16, hi) + _dot(m_bf16, mid) + _dot(m_bf16, lo)


def _dot3_r(x, m_bf16):
    hi, mid, lo = _split3(x)
    return _dot(hi, m_bf16) + _dot(mid, m_bf16) + _dot(lo, m_bf16)


def _iota_mask(shape, fn):
    r = lax.broadcasted_iota(jnp.int32, shape, 0)
    c = lax.broadcasted_iota(jnp.int32, shape, 1)
    return fn(r, c)


def _tri_bf16(shape, fn):
    return jnp.where(_iota_mask(shape, fn), 1.0, 0.0).astype(BF16)


def _rms(x, g):
    return x * lax.rsqrt(jnp.mean(x * x, axis=-1, keepdims=True) + EPS) * g


def _sigmoid(x):
    return 1.0 / (1.0 + jnp.exp(-x))


def _softplus(x):
    return jnp.maximum(x, 0.0) + jnp.log1p(jnp.exp(-jnp.abs(x)))


def _lane_expand(cols, lane_head):
    out = jnp.where(lane_head == 2, cols[2], cols[3])
    out = jnp.where(lane_head == 1, cols[1], out)
    return jnp.where(lane_head == 0, cols[0], out)


def _in_proj_kernel(x_ref, g_ref, w_ref, o_ref):
    xn = _rms(x_ref[...], g_ref[...])
    o_ref[...] = _dot(xn.astype(BF16), w_ref[...])


def _in_proj(x, g, w, tm):
    n = x.shape[0]
    return pl.pallas_call(
        _in_proj_kernel,
        grid=(n // tm,),
        in_specs=[pl.BlockSpec((tm, D_MODEL), lambda i: (i, 0)),
                  _const_spec((1, D_MODEL)),
                  _const_spec((D_MODEL, PROJ_W))],
        out_specs=pl.BlockSpec((tm, PROJ_W), lambda i: (i, 0)),
        out_shape=jax.ShapeDtypeStruct((n, PROJ_W), F32),
        compiler_params=_cparams(("parallel",)),
        name="in_proj",
    )(x, g, w)


def _logf_cum_kernel(m_ref, b_ref, logf_ref, cum_ref, carry_ref, *, lc):
    @pl.when(pl.program_id(1) == 0)
    def _():
        carry_ref[...] = jnp.zeros_like(carry_ref)

    x = m_ref[...] + b_ref[...]
    logf = jnp.minimum(x, 0.0) - jnp.log1p(jnp.exp(-jnp.abs(x)))
    tri = _tri_bf16((lc, lc), lambda r, c: c <= r)
    cum = _dot3_l(tri, logf) + carry_ref[...]
    logf_ref[...] = logf
    cum_ref[...] = cum
    carry_ref[...] = cum[lc - 1:lc, :]


def _logf_cum(proj, bias, nseq, t, lc):
    nc = t // lc
    n = nseq * t
    return pl.pallas_call(
        functools.partial(_logf_cum_kernel, lc=lc),
        grid=(nseq, nc),
        in_specs=[pl.BlockSpec((lc, 128), lambda b, c: (b * nc + c, MISC_BLK)),
                  _const_spec((1, 128))],
        out_specs=[pl.BlockSpec((lc, 128), lambda b, c: (b * nc + c, 0)),
                   pl.BlockSpec((lc, 128), lambda b, c: (b * nc + c, 0))],
        out_shape=[jax.ShapeDtypeStruct((n, 128), F32), jax.ShapeDtypeStruct((n, 128), F32)],
        scratch_shapes=[pltpu.VMEM((1, 128), F32)],
        compiler_params=_cparams(("parallel", "arbitrary")),
        name="logf_cum",
    )(proj, bias)


AUG_W = HEADS * 128
LOG2E = math.log2(math.e)


def _bias_lanes(c, first):
    lane = lax.broadcasted_iota(jnp.int32, (1, AUG_W), 1)
    blk = lane // 128
    off = lane % 128 - HEAD_DIM
    hi, mid, lo = [t.astype(F32) for t in _split3(c)]
    out = jnp.where((off >= 3 - first) & (off < 6 - first), 1.0, 0.0)
    for j, t in enumerate((hi, mid, lo)):
        cols = [t[:, h:h + 1] for h in range(HEADS)]
        tl = jnp.where(blk == 2, cols[2], cols[3])
        tl = jnp.where(blk == 1, cols[1], tl)
        tl = jnp.where(blk == 0, cols[0], tl)
        out = jnp.where(off == first + j, tl, out)
    return out


def _fox_prep_kernel(m_ref, q_ref, k_ref, v_ref, b_ref, logf_ref, qa_ref, ka_ref, kt_ref, vt_ref,
                     vtb_ref, carry_ref, *, lc):
    @pl.when(pl.program_id(1) == 0)
    def _():
        carry_ref[...] = jnp.zeros_like(carry_ref)

    x = m_ref[...] + b_ref[...]
    logf = jnp.minimum(x, 0.0) - jnp.log1p(jnp.exp(-jnp.abs(x)))
    tri = _tri_bf16((lc, lc), lambda r, c: c <= r)
    cum = _dot3_l(tri, logf) + carry_ref[...]
    logf_ref[...] = logf
    carry_ref[...] = cum[lc - 1:lc, :]

    place = _tri_bf16((GROUP_W, AUG_W),
                      lambda r, c: (c // 128 == r // HEAD_DIM) & (c % 128 == r % HEAD_DIM))
    c2 = cum * LOG2E
    q = (q_ref[...] * (FOX_SCALE * LOG2E)).astype(BF16)
    qa_ref[...] = (_dot(q, place) + _bias_lanes(c2, 0)).astype(BF16)
    k = k_ref[...]
    ka_ref[...] = (_dot(k.astype(BF16), place) + _bias_lanes(-c2, 3)).astype(BF16)
    kt_ref[...] = jnp.transpose(k)
    vt = jnp.transpose(v_ref[...])
    vt_ref[...] = vt
    vtb_ref[...] = vt.astype(BF16)


def _fox_prep(proj, bias, nseq, t, lc):
    nc = t // lc
    n = nseq * t
    rows = lambda w, cb: pl.BlockSpec((lc, w), lambda b, c: (b * nc + c, cb))
    chan = pl.BlockSpec((None, GROUP_W, lc), lambda b, c: (b, 0, c))
    return pl.pallas_call(
        functools.partial(_fox_prep_kernel, lc=lc),
        grid=(nseq, nc),
        in_specs=[rows(128, MISC_BLK), rows(GROUP_W, COL_Q), rows(GROUP_W, COL_K), rows(GROUP_W, COL_V),
                  _const_spec((1, 128))],
        out_specs=[rows(128, 0), rows(AUG_W, 0), rows(AUG_W, 0), chan, chan, chan],
        out_shape=[jax.ShapeDtypeStruct((n, 128), F32), jax.ShapeDtypeStruct((n, AUG_W), BF16),
                   jax.ShapeDtypeStruct((n, AUG_W), BF16),
                   jax.ShapeDtypeStruct((nseq, GROUP_W, t), F32),
                   jax.ShapeDtypeStruct((nseq, GROUP_W, t), F32),
                   jax.ShapeDtypeStruct((nseq, GROUP_W, t), BF16)],
        scratch_shapes=[pltpu.VMEM((1, 128), F32)],
        compiler_params=_cparams(("parallel", "arbitrary")),
        name="fox_prep",
    )(proj, proj, proj, proj, bias)


def _s5_kernel(u_ref, h0_ref, bri_ref, ak_ref, ct_ref, d_ref, wg_ref, g_ref,
               y_ref, st_ref, bu_ref, xc_ref, *, tc, state_row):
    c = pl.program_id(1)

    @pl.when(c == 0)
    def _():
        h0 = h0_ref[...]
        xc_ref[0] = jnp.broadcast_to(h0[:, :S5_CH], (8, S5_CH))
        xc_ref[1] = jnp.broadcast_to(h0[:, S5_CH:], (8, S5_CH))

    u = u_ref[...]
    bu_ref[...] = _dot(u.astype(BF16), bri_ref[...])

    def tile(i, carry):
        xr_c, xi_c = carry
        r0 = pl.multiple_of(i * 8, 8)
        dr = bu_ref[pl.ds(r0, 8), 0:S5_CH]
        di = bu_ref[pl.ds(r0, 8), S5_CH:2 * S5_CH]
        for j, k in enumerate((1, 2, 4)):
            ar = ak_ref[2 + 2 * j]
            ai = ak_ref[3 + 2 * j]
            sr = pltpu.roll(dr, k, axis=0)
            si = pltpu.roll(di, k, axis=0)
            dr, di = dr + ar * sr - ai * si, di + ar * si + ai * sr
        pr = ak_ref[0]
        pi = ak_ref[1]
        xr = dr + pr * xr_c - pi * xi_c
        xi = di + pr * xi_c + pi * xr_c
        bu_ref[pl.ds(r0, 8), 0:S5_CH] = xr
        bu_ref[pl.ds(r0, 8), S5_CH:2 * S5_CH] = xi
        return (jnp.broadcast_to(xr[7:8, :], (8, S5_CH)),
                jnp.broadcast_to(xi[7:8, :], (8, S5_CH)))

    xr_c, xi_c = lax.fori_loop(0, tc // 8, tile, (xc_ref[0], xc_ref[1]))
    xc_ref[0] = xr_c
    xc_ref[1] = xi_c

    @pl.when(c == pl.num_programs(1) - 1)
    def _():
        st_ref[...] = bu_ref[state_row:state_row + 1, :]

    y = _dot(bu_ref[...].astype(BF16), ct_ref[...]) + d_ref[...] * u
    y = 0.5 * y * (1.0 + jnp.tanh(math.sqrt(2.0 / math.pi) * (y + 0.044715 * (y * y * y))))
    y = y * _sigmoid(_dot(y.astype(BF16), wg_ref[...]))
    y_ref[...] = _rms(y, g_ref[...])


def _s5(proj, h0, bri, ak, ct, d, wg, g, nseq, t, tc, state_row):
    nc = t // tc
    n = nseq * t
    return pl.pallas_call(
        functools.partial(_s5_kernel, tc=tc, state_row=state_row),
        grid=(nseq, nc),
        in_specs=[pl.BlockSpec((tc, GROUP_W), lambda b, c: (b * nc + c, COL_U)),
                  pl.BlockSpec((None, 1, 2 * S5_CH), lambda b, c: (b, 0, 0)),
                  _const_spec((GROUP_W, 2 * S5_CH)),
                  _const_spec((8, 8, S5_CH)),
                  _const_spec((2 * S5_CH, GROUP_W)),
                  _const_spec((1, GROUP_W)),
                  _const_spec((GROUP_W, GROUP_W)),
                  _const_spec((1, GROUP_W))],
        out_specs=[pl.BlockSpec((tc, GROUP_W), lambda b, c: (b * nc + c, 0)),
                   pl.BlockSpec((None, 1, 2 * S5_CH), lambda b, c: (b, 0, 0))],
        out_shape=[jax.ShapeDtypeStruct((n, GROUP_W), F32),
                   jax.ShapeDtypeStruct((nseq, 1, 2 * S5_CH), F32)],
        scratch_shapes=[pltpu.VMEM((tc, 2 * S5_CH), F32), pltpu.VMEM((2, 8, S5_CH), F32)],
        compiler_params=_cparams(("parallel", "arbitrary")),
        name="s5",
    )(proj, h0, bri, ak, ct, d, wg, g)


def _fox_prompt_kernel(qi_ref, ki_ref, q_ref, k_ref, vt_ref, g_ref, o_ref,
                       m_ref, l_ref, acc_ref, *, tq, tk):
    step = pl.program_id(1)
    qi = qi_ref[step]
    ki = ki_ref[step]

    @pl.when(ki == 0)
    def _():
        m_ref[...] = jnp.full_like(m_ref, NEG)
        l_ref[...] = jnp.zeros_like(l_ref)
        acc_ref[...] = jnp.zeros_like(acc_ref)

    def update(masked):
        if masked:
            causal = _iota_mask((tk, tq), lambda r, c: r <= c)
        for h in range(HEADS):
            s = _dot_nt(k_ref[:, h * 128:(h + 1) * 128], q_ref[:, h * 128:(h + 1) * 128])
            if masked:
                s = jnp.where(causal, s, NEG)
            m_old = m_ref[h]
            m_new = jnp.maximum(m_old, jnp.max(s, axis=0, keepdims=True))
            alpha = jnp.exp2(m_old - m_new)
            p = jnp.exp2(s - m_new)
            l_ref[h] = alpha * l_ref[h] + jnp.sum(p, axis=0, keepdims=True)
            pair = (h // 2) * 128
            acc_ref[h] = alpha * acc_ref[h] + _dot(vt_ref[pair:pair + 128, :], p.astype(BF16))
            m_ref[h] = m_new

    @pl.when(ki < qi)
    def _():
        update(False)

    @pl.when(ki == qi)
    def _():
        update(True)
        parts = []
        for h in range(HEADS):
            r0 = (h % 2) * HEAD_DIM
            parts.append(acc_ref[h][r0:r0 + HEAD_DIM, :] / l_ref[h])
        ot = jnp.concatenate(parts, axis=0)
        ot = ot * lax.rsqrt(jnp.mean(ot * ot, axis=0, keepdims=True) + EPS) * g_ref[...]
        o_ref[...] = jnp.transpose(ot)


def _fox_prompt(qa, ka, vtb, g, nseq, t, tq):
    tk = tq
    nq = t // tq
    n = nseq * t
    pairs = [(i, j) for i in range(nq) for j in range(i + 1)]
    qi_tab = jnp.asarray([p[0] for p in pairs], jnp.int32)
    ki_tab = jnp.asarray([p[1] for p in pairs], jnp.int32)
    grid_spec = pltpu.PrefetchScalarGridSpec(
        num_scalar_prefetch=2,
        grid=(nseq, len(pairs)),
        in_specs=[pl.BlockSpec((tq, AUG_W), lambda b, s, qt, kt: (b * nq + qt[s], 0)),
                  pl.BlockSpec((tk, AUG_W), lambda b, s, qt, kt: (b * nq + kt[s], 0)),
                  pl.BlockSpec((None, GROUP_W, tk), lambda b, s, qt, kt: (b, 0, kt[s])),
                  pl.BlockSpec((GROUP_W, 1), lambda b, s, qt, kt: (0, 0))],
        out_specs=pl.BlockSpec((tq, GROUP_W), lambda b, s, qt, kt: (b * nq + qt[s], 0)),
        scratch_shapes=[pltpu.VMEM((HEADS, 1, tq), F32),
                        pltpu.VMEM((HEADS, 1, tq), F32),
                        pltpu.VMEM((HEADS, 128, tq), F32)],
    )
    return pl.pallas_call(
        functools.partial(_fox_prompt_kernel, tq=tq, tk=tk),
        grid_spec=grid_spec,
        out_shape=jax.ShapeDtypeStruct((n, GROUP_W), F32),
        compiler_params=_cparams(("parallel", "arbitrary")),
        name="fox_prompt",
    )(qi_tab, ki_tab, qa, ka, vtb, g)


PAGE_ROWS = PAGE * HEADS


def _past_bias_kernel(pt_ref, pool_ref, o_ref, lbuf_ref, sem, *, layer, n_pages):
    b = pl.program_id(0)

    def page_copy(p):
        return pltpu.make_async_copy(pool_ref.at[layer, pt_ref[b * n_pages + p]],
                                     lbuf_ref.at[pl.ds(p * 8, 8)], sem.at[0])

    def start(p, _):
        page_copy(p).start()
        return 0

    def wait(p, _):
        page_copy(p).wait()
        return 0

    lax.fori_loop(0, n_pages, start, 0)
    lax.fori_loop(0, n_pages, wait, 0)

    x = lbuf_ref[...]
    rows = n_pages * 8
    after = _tri_bf16((PAGE, PAGE), lambda r, c: r > c)
    local = _dot3_r(x, after)
    ones = jnp.ones((PAGE, 128), BF16)
    tot = _dot3_r(x, ones)
    later = _tri_bf16((rows, rows), lambda r, c: (c % 8 == r % 8) & (c // 8 > r // 8))
    res = (local + _dot3_l(later, tot)) * LOG2E
    for p in range(n_pages):
        o_ref[:, p * PAGE:(p + 1) * PAGE] = res[p * 8:(p + 1) * 8, :]


def _past_bias(page_table_flat, pool_t, layer, nseq, n_pages):
    rows = n_pages * 8
    grid_spec = pltpu.PrefetchScalarGridSpec(
        num_scalar_prefetch=1,
        grid=(nseq,),
        in_specs=[pl.BlockSpec(memory_space=pl.ANY)],
        out_specs=pl.BlockSpec((None, 8, n_pages * PAGE), lambda b, pt: (b, 0, 0)),
        scratch_shapes=[pltpu.VMEM((rows, 128), F32), pltpu.SemaphoreType.DMA((1,))],
    )
    return pl.pallas_call(
        functools.partial(_past_bias_kernel, layer=layer, n_pages=n_pages),
        grid_spec=grid_spec,
        out_shape=jax.ShapeDtypeStruct((nseq, 8, n_pages * PAGE), F32),
        compiler_params=_cparams(("arbitrary",)),
        name="past_bias",
    )(page_table_flat, pool_t)


def _fox_sample_kernel(pt_ref, ck_ref, cv_ref, q_ref, kn_ref, vn_ref, rt_ref, ln_ref,
                       o_ref, kbuf, vbuf, sem, m_ref, l_ref, acc_ref,
                       *, layer, n_pages, pg, dec_t):
    g = pl.program_id(0)
    ng = pl.num_programs(0)
    nc = n_pages // pg
    c = g % nc
    slot = g % 2

    def copies(step, sl):
        out = []
        for j in range(pg):
            pid = pt_ref[step * pg + j]
            out.append(pltpu.make_async_copy(ck_ref.at[layer, pid], kbuf.at[sl, j], sem.at[0, sl]))
            out.append(pltpu.make_async_copy(cv_ref.at[layer, pid], vbuf.at[sl, j], sem.at[1, sl]))
        return out

    @pl.when(g == 0)
    def _():
        for cp in copies(0, 0):
            cp.start()

    @pl.when(g + 1 < ng)
    def _():
        for cp in copies(g + 1, 1 - slot):
            cp.start()

    for cp in copies(g, slot):
        cp.wait()

    row_tok = lax.broadcasted_iota(jnp.int32, (QROWS, 1), 0) % 8

    @pl.when(c == 0)
    def _():
        m_ref[...] = jnp.full_like(m_ref, NEG)
        l_ref[...] = jnp.zeros_like(l_ref)
        acc_ref[...] = jnp.zeros_like(acc_ref)

    ln = ln_ref[...] * LOG2E
    cums = [ln[:, 0:1]]
    for t in range(1, dec_t):
        cums.append(cums[-1] + ln[:, t:t + 1])
    pcol = cums[dec_t - 1]
    for t in range(dec_t - 2, -1, -1):
        pcol = jnp.where(row_tok == t, cums[t], pcol)

    qbd = q_ref[...].astype(BF16)

    def attend(s, vts):
        m_old = m_ref[...]
        m_new = jnp.maximum(m_old, jnp.max(s, axis=1, keepdims=True))
        alpha = jnp.exp2(m_old - m_new)
        p = jnp.exp2(s - m_new)
        l_ref[...] = alpha * l_ref[...] + jnp.sum(p, axis=1, keepdims=True)
        pb = p.astype(BF16)
        pv = [_dot_nt(pb[:, j * PAGE:(j + 1) * PAGE], vt) for j, vt in enumerate(vts)]
        while len(pv) > 1:
            pv = [a + b for a, b in zip(pv[0::2], pv[1::2])]
        acc_ref[...] = alpha * acc_ref[...] + pv[0]
        m_ref[...] = m_new

    rt = rt_ref[...]
    bias = jnp.concatenate([jnp.broadcast_to(rt[h:h + 1, :], (8, pg * PAGE)) for h in range(HEADS)], axis=0)
    s = jnp.concatenate([_dot(qbd, kbuf[slot, j].astype(BF16)) for j in range(pg)], axis=1)
    attend(s + bias + pcol, [vbuf[slot, j].astype(BF16) for j in range(pg)])

    @pl.when(c == nc - 1)
    def _():
        lane = lax.broadcasted_iota(jnp.int32, (1, PAGE), 1)
        pt_l = jnp.broadcast_to(cums[dec_t - 1], (QROWS, PAGE))
        for t in range(dec_t - 2, -1, -1):
            pt_l = jnp.where(lane == t, cums[t], pt_l)
        s_new = _dot(qbd, kn_ref[...].astype(BF16)) + pcol - pt_l
        s_new = jnp.where((lane < dec_t) & (lane <= row_tok), s_new, NEG)
        attend(s_new, [vn_ref[...].astype(BF16)])
        o_ref[...] = acc_ref[...] / l_ref[...]


QROWS = HEADS * 8


def _fox_sample(pt_flat, ckt, cvt, qbd, knt, vnt, rt, ln, layer, nseq, n_pages, pg, dec_t):
    nc = n_pages // pg
    hd = HEADS * HEAD_DIM
    per_seq = lambda shape: pl.BlockSpec((None,) + shape, lambda g, pt: (g // nc,) + (0,) * len(shape))
    grid_spec = pltpu.PrefetchScalarGridSpec(
        num_scalar_prefetch=1,
        grid=(nseq * nc,),
        in_specs=[pl.BlockSpec(memory_space=pl.ANY),
                  pl.BlockSpec(memory_space=pl.ANY),
                  per_seq((QROWS, hd)),
                  per_seq((hd, PAGE)),
                  per_seq((hd, PAGE)),
                  pl.BlockSpec((None, 8, pg * PAGE), lambda g, pt: (g // nc, 0, g % nc)),
                  per_seq((QROWS, 128))],
        out_specs=per_seq((QROWS, hd)),
        scratch_shapes=[pltpu.VMEM((2, pg, hd, PAGE), F32),
                        pltpu.VMEM((2, pg, hd, PAGE), F32),
                        pltpu.SemaphoreType.DMA((2, 2)),
                        pltpu.VMEM((QROWS, 1), F32),
                        pltpu.VMEM((QROWS, 1), F32),
                        pltpu.VMEM((QROWS, hd), F32)],
    )
    return pl.pallas_call(
        functools.partial(_fox_sample_kernel, layer=layer, n_pages=n_pages, pg=pg, dec_t=dec_t),
        grid_spec=grid_spec,
        out_shape=jax.ShapeDtypeStruct((nseq, QROWS, hd), F32),
        compiler_params=_cparams(("arbitrary",)),
        name="fox_sample",
    )(pt_flat, ckt, cvt, qbd, knt, vnt, rt, ln)


def _rms_rows_kernel(x_ref, g_ref, o_ref):
    o_ref[...] = _rms(x_ref[...], g_ref[...])


def _rms_rows(x, g):
    return pl.pallas_call(
        _rms_rows_kernel,
        out_shape=jax.ShapeDtypeStruct(x.shape, F32),
        name="rms_rows",
    )(x, g)


def _ssd_kernel(z_ref, xs_ref, b_ref, c_ref, sb_ref, sc_ref, sh_ref, misc_ref,
                h0_ref, cb0_ref, sb0_ref,
                cw_ref, cbias_ref, dtb_ref, alog_ref, dskip_ref, ng_ref, scw_ref, scg_ref,
                yc_ref, yd_ref, hout_ref, cbout_ref, sbout_ref,
                ext_ref, exts_ref, h_ref, *, L, valid):
    c = pl.program_id(1)
    last = pl.num_programs(1) - 1
    XW = 3 * GROUP_W

    @pl.when(c == 0)
    def _():
        h_ref[...] = h0_ref[...]
        ext_ref[8 - (SSD_CONV - 1):8, :] = cb0_ref[...]
        exts_ref[8 - (SC_CONV - 1):8, :] = sb0_ref[...]

    ext_ref[8:8 + L, 0:GROUP_W] = xs_ref[...]
    ext_ref[8:8 + L, GROUP_W:2 * GROUP_W] = b_ref[...]
    ext_ref[8:8 + L, 2 * GROUP_W:XW] = c_ref[...]
    exts_ref[8:8 + L, :] = sc_ref[...] * sh_ref[...]

    cw = cw_ref[...]
    acc = cbias_ref[...] + cw[0:1, :] * ext_ref[5:5 + L, :]
    for j in range(1, SSD_CONV):
        acc = acc + cw[j:j + 1, :] * ext_ref[5 + j:5 + j + L, :]
    xbc = acc * _sigmoid(acc)
    xs = xbc[:, 0:GROUP_W]
    bm = xbc[:, GROUP_W:2 * GROUP_W].astype(BF16)
    cm = xbc[:, 2 * GROUP_W:XW].astype(BF16)

    scw = scw_ref[...]
    conv = scw[0:1, :] * exts_ref[6:6 + L, :]
    for j in range(1, SC_CONV):
        conv = conv + scw[j:j + 1, :] * exts_ref[6 + j:6 + j + L, :]
    yd_ref[...] = _rms(sb_ref[...] * conv, scg_ref[...])

    @pl.when(c == last)
    def _():
        cbout_ref[...] = ext_ref[8 + valid - (SSD_CONV - 1):8 + valid, :]
        sbout_ref[...] = exts_ref[8 + valid - (SC_CONV - 1):8 + valid, :]

    ext_ref[8 - (SSD_CONV - 1):8, :] = ext_ref[8 + L - (SSD_CONV - 1):8 + L, :]
    exts_ref[8 - (SC_CONV - 1):8, :] = exts_ref[8 + L - (SC_CONV - 1):8 + L, :]

    lane128 = lax.broadcasted_iota(jnp.int32, (1, 128), 1)
    rowi = lax.broadcasted_iota(jnp.int32, (L, 1), 0)
    live = (lane128 >= DT_LANE) & (lane128 < DT_LANE + HEADS) & (rowi < valid)
    dt = jnp.where(live, _softplus(misc_ref[...] + dtb_ref[...]), 0.0)
    a = dt * (-jnp.exp(alog_ref[...]))
    incl = _tri_bf16((L, L), lambda r, cc: cc <= r)
    after = _tri_bf16((L, L), lambda r, cc: cc > r)
    acum = _dot3_l(incl, a)
    rev = _dot3_l(after, a)
    lower = _iota_mask((L, L), lambda r, cc: cc <= r)
    strict_f = jnp.where(_iota_mask((L, L), lambda r, cc: r > cc), 1.0, 0.0)

    lane_head = lax.broadcasted_iota(jnp.int32, (1, GROUP_W), 1) // HEAD_DIM
    col = lambda arr, h: arr[:, DT_LANE + h:DT_LANE + h + 1]
    dt_l = _lane_expand([col(dt, h) for h in range(HEADS)], lane_head)
    eac_l = _lane_expand([jnp.exp(col(acum, h)) for h in range(HEADS)], lane_head)
    erev_l = _lane_expand([jnp.exp(col(rev, h)) for h in range(HEADS)], lane_head)
    xdt = xs * dt_l
    xdt_b = xdt.astype(BF16)

    gmat = [_dot_nt(cm[:, g * SSD_STATE:(g + 1) * SSD_STATE], bm[:, g * SSD_STATE:(g + 1) * SSD_STATE])
            for g in range(2)]
    y = jnp.zeros((L, GROUP_W), F32)
    for h in range(HEADS):
        delta = _dot3_l(incl, col(a, h) * strict_f)
        lmat = jnp.where(lower, jnp.exp(delta), 0.0)
        scores = (gmat[h // 2] * lmat).astype(BF16)
        y = jnp.where(lane_head == h, _dot(scores, xdt_b), y)

    h_all = h_ref[...]
    hb = h_all.astype(BF16)
    y0 = _dot_nt(cm[:, 0:SSD_STATE], hb)
    y1 = _dot_nt(cm[:, SSD_STATE:2 * SSD_STATE], hb)
    y = y + jnp.concatenate([y0[:, 0:128], y1[:, 128:256]], axis=1) * eac_l

    wt = jnp.transpose(xdt * erev_l).astype(BF16)
    s0 = _dot(wt, bm[:, 0:SSD_STATE])
    s1 = _dot(wt, bm[:, SSD_STATE:2 * SSD_STATE])
    upd = jnp.concatenate([s0[0:128, :], s1[128:256, :]], axis=0)
    row_head = lax.broadcasted_iota(jnp.int32, (GROUP_W, 1), 0) // HEAD_DIM
    dec = [jnp.exp(acum[L - 1:L, DT_LANE + h:DT_LANE + h + 1]) for h in range(HEADS)]
    scale = jnp.where(row_head == 2, dec[2], dec[3])
    scale = jnp.where(row_head == 1, dec[1], scale)
    scale = jnp.where(row_head == 0, dec[0], scale)
    h_new = scale * h_all + upd
    h_ref[...] = h_new

    @pl.when(c == last)
    def _():
        hout_ref[...] = h_new

    y = y + dskip_ref[...] * xs
    zz = z_ref[...]
    y = y * (zz * _sigmoid(zz))
    yc_ref[...] = _rms(y, ng_ref[...])


def _ssd(proj, h0, cb0, sb0, cw, cbias, dtb, alog, dskip, ng, scw, scg, nseq, t, L, valid):
    nc = t // L
    n = nseq * t
    XW = 3 * GROUP_W

    def col_spec(cb):
        return pl.BlockSpec((L, GROUP_W), lambda b, c: (b * nc + c, cb))

    state_specs = [pl.BlockSpec((None, GROUP_W, SSD_STATE), lambda b, c: (b, 0, 0)),
                   pl.BlockSpec((None, SSD_CONV - 1, XW), lambda b, c: (b, 0, 0)),
                   pl.BlockSpec((None, SC_CONV - 1, GROUP_W), lambda b, c: (b, 0, 0))]
    return pl.pallas_call(
        functools.partial(_ssd_kernel, L=L, valid=valid),
        grid=(nseq, nc),
        in_specs=[col_spec(COL_Z), col_spec(COL_XS), col_spec(COL_B), col_spec(COL_C),
                  col_spec(COL_SB), col_spec(COL_SC), col_spec(COL_SH),
                  pl.BlockSpec((L, 128), lambda b, c: (b * nc + c, MISC_BLK))]
                 + state_specs
                 + [_const_spec((SSD_CONV, XW)), _const_spec((1, XW)), _const_spec((1, 128)),
                    _const_spec((1, 128)), _const_spec((1, GROUP_W)), _const_spec((1, GROUP_W)),
                    _const_spec((SC_CONV, GROUP_W)), _const_spec((1, GROUP_W))],
        out_specs=[pl.BlockSpec((L, GROUP_W), lambda b, c: (b * nc + c, 0)),
                   pl.BlockSpec((L, GROUP_W), lambda b, c: (b * nc + c, 0))] + state_specs,
        out_shape=[jax.ShapeDtypeStruct((n, GROUP_W), F32),
                   jax.ShapeDtypeStruct((n, GROUP_W), F32),
                   jax.ShapeDtypeStruct((nseq, GROUP_W, SSD_STATE), F32),
                   jax.ShapeDtypeStruct((nseq, SSD_CONV - 1, XW), F32),
                   jax.ShapeDtypeStruct((nseq, SC_CONV - 1, GROUP_W), F32)],
        scratch_shapes=[pltpu.VMEM((8 + L, XW), F32), pltpu.VMEM((8 + L, GROUP_W), F32),
                        pltpu.VMEM((GROUP_W, SSD_STATE), F32)],
        compiler_params=_cparams(("parallel", "arbitrary")),
        name="ssd",
    )(proj, proj, proj, proj, proj, proj, proj, proj, h0, cb0, sb0,
      cw, cbias, dtb, alog, dskip, ng, scw, scg)


def _out_ffn_kernel(x_ref, ya_ref, yb_ref, yc_ref, yd_ref, wo_ref, g_ref, wu_ref, wd_ref, gf_ref,
                    *out_refs, final):
    x1 = x_ref[...]
    for i, y_ref in enumerate((ya_ref, yb_ref, yc_ref, yd_ref)):
        x1 = x1 + _dot(y_ref[...].astype(BF16), wo_ref[i * GROUP_W:(i + 1) * GROUP_W, :])
    hn = _rms(x1, g_ref[...]).astype(BF16)
    fc = 1024
    parts = []
    for j in range(FFN // fc):
        hj = jnp.maximum(_dot(hn, wu_ref[:, j * fc:(j + 1) * fc]), 0.0)
        parts.append(_dot((hj * hj).astype(BF16), wd_ref[j * fc:(j + 1) * fc, :]))
    acc = x1 + ((parts[0] + parts[1]) + (parts[2] + parts[3]))
    out_refs[0][...] = acc
    if final:
        out_refs[1][...] = _rms(acc, gf_ref[...])


def _out_ffn(x, ya, yb, yc, yd, wo, g, wu, wd, gf, tm, final):
    n = x.shape[0]
    row = lambda w: pl.BlockSpec((tm, w), lambda i: (i, 0))
    n_out = 2 if final else 1
    return pl.pallas_call(
        functools.partial(_out_ffn_kernel, final=final),
        grid=(n // tm,),
        in_specs=[row(D_MODEL), row(GROUP_W), row(GROUP_W), row(GROUP_W), row(GROUP_W),
                  _const_spec((D_MODEL, D_MODEL)), _const_spec((1, D_MODEL)),
                  _const_spec((D_MODEL, FFN)), _const_spec((FFN, D_MODEL)),
                  _const_spec((1, D_MODEL))],
        out_specs=[row(D_MODEL)] * n_out,
        out_shape=[jax.ShapeDtypeStruct((n, D_MODEL), F32)] * n_out,
        compiler_params=_cparams(("parallel",)),
        name="out_ffn",
    )(x, ya, yb, yc, yd, wo, g, wu, wd, gf)


def _s5_params(lam_re, lam_im, log_dt, b_re, b_im, c_re, c_im):
    dt = jnp.exp(log_dt)[:, None]
    mag = jnp.exp(lam_re * dt)
    abr, abi = mag * jnp.cos(lam_im * dt), mag * jnp.sin(lam_im * dt)
    den = lam_re * lam_re + lam_im * lam_im
    qr = ((abr - 1.0) * lam_re + abi * lam_im) / den
    qi = (abi * lam_re - (abr - 1.0) * lam_im) / den
    bbr = qr[..., None] * b_re - qi[..., None] * b_im
    bbi = qr[..., None] * b_im + qi[..., None] * b_re
    eye = jnp.eye(S5_G, dtype=F32)
    bd = lambda m: jnp.einsum('gph,gk->ghkp', m, eye).reshape(S5_G * S5_H, S5_CH)
    bri = jnp.concatenate([bd(bbr), bd(bbi)], axis=1).astype(BF16)
    cd = lambda m: jnp.einsum('ghp,gk->gpkh', m, eye).reshape(S5_CH, S5_G * S5_H)
    ct = jnp.concatenate([cd(c_re), -cd(c_im)], axis=0).astype(BF16)
    ar, ai = abr.reshape(1, S5_CH), abi.reshape(1, S5_CH)
    pows = [(ar, ai)]
    for _ in range(7):
        pr, pi = pows[-1]
        pows.append((pr * ar - pi * ai, pr * ai + pi * ar))
    rows = jnp.arange(8)[:, None]
    tabs = [jnp.concatenate([p[0] for p in pows], axis=0), jnp.concatenate([p[1] for p in pows], axis=0)]
    for k in (1, 2, 4):
        for part in pows[k - 1]:
            tabs.append(jnp.where(rows >= k, part, 0.0))
    return bri, jnp.stack(tabs, axis=0), ct


def _pad_lanes(v, off):
    return jnp.zeros((1, 128), F32).at[0, off:off + v.shape[0]].set(v)


def _permute_w_in(w):
    return jnp.concatenate(
        [w[:, 0:1024], w[:, 1028:2052], w[:, 2056:2824], w[:, 1024:1028], w[:, 2052:2056],
         jnp.zeros((w.shape[0], 120), w.dtype)], axis=1).astype(BF16)


def kernel(x_prompt, x_sample, cache_k, cache_v, cache_logf, state_s5, state_ssd, state_ssd_conv, state_sconv, page_table, norm_mix_g, w_in, s5_lam_re, s5_lam_im, s5_log_dt, s5_b_re, s5_b_im, s5_c_re, s5_c_im, s5_d, s5_w_glu, s5_norm_g, fox_b_f, fox_norm_g, ssd_conv_w, ssd_conv_b, ssd_dt_bias, ssd_a_log, ssd_d, ssd_norm_g, sc_conv_w, sc_norm_g, w_out, norm_ffn_g, w_up, w_down, norm_final_g):
    bp, tp, _ = x_prompt.shape
    bs, ts, _ = x_sample.shape
    n_pool = cache_k.shape[1]
    n_pages = page_table.shape[1]
    np_tok, ns_tok = bp * tp, bs * ts
    pad_t = 128

    pt_flat = page_table.reshape(-1)
    ckt = jnp.transpose(cache_k, (0, 1, 3, 4, 2)).reshape(DEPTH, n_pool, GROUP_W, PAGE)
    cvt = jnp.transpose(cache_v, (0, 1, 3, 4, 2)).reshape(DEPTH, n_pool, GROUP_W, PAGE)
    pool_t = jnp.pad(jnp.swapaxes(cache_logf, 2, 3), ((0, 0), (0, 0), (0, 8 - HEADS), (0, 0)))
    gfin = norm_final_g.reshape(1, D_MODEL)

    hp = x_prompt.reshape(np_tok, D_MODEL)
    hs = x_sample.reshape(ns_tok, D_MODEL)
    zeros_p = dict(
        s5=jnp.zeros((bp, 1, 2 * S5_CH), F32), ssd=jnp.zeros((bp, GROUP_W, SSD_STATE), F32),
        cb=jnp.zeros((bp, SSD_CONV - 1, 3 * GROUP_W), F32), sb=jnp.zeros((bp, SC_CONV - 1, GROUP_W), F32))
    outs_p, outs_s = [], []
    y_prompt = y_sample = None

    for l in range(DEPTH):
        final = l == DEPTH - 1
        w_in_l = _permute_w_in(w_in[l])
        g_mix = norm_mix_g[l].reshape(1, D_MODEL)
        bri, ak, ct = _s5_params(s5_lam_re[l], s5_lam_im[l], s5_log_dt[l], s5_b_re[l], s5_b_im[l],
                                 s5_c_re[l], s5_c_im[l])
        s5_args = (bri, ak, ct, s5_d[l].reshape(1, GROUP_W), s5_w_glu[l].astype(BF16),
                   s5_norm_g[l].reshape(1, GROUP_W))
        fb = _pad_lanes(fox_b_f[l], 0)
        fg = fox_norm_g[l].reshape(1, GROUP_W)
        ssd_args = (ssd_conv_w[l], ssd_conv_b[l].reshape(1, -1), _pad_lanes(ssd_dt_bias[l], DT_LANE),
                    _pad_lanes(ssd_a_log[l], DT_LANE), jnp.repeat(ssd_d[l], HEAD_DIM).reshape(1, GROUP_W),
                    ssd_norm_g[l].reshape(1, GROUP_W), sc_conv_w[l], sc_norm_g[l].reshape(1, GROUP_W))
        ffn_args = (w_out[l].astype(BF16), norm_ffn_g[l].reshape(1, D_MODEL), w_up[l].astype(BF16),
                    w_down[l].astype(BF16), gfin)

        proj = _in_proj(hp, g_mix, w_in_l, 512)
        logf, qa, ka, kt, vt, vtb = _fox_prep(proj, fb, bp, tp, 512)
        ya, st5 = _s5(proj, zeros_p['s5'], *s5_args, bp, tp, 512, 511)
        yb = _fox_prompt(qa, ka, vtb, fox_norm_g[l].reshape(GROUP_W, 1), bp, tp, 512)
        yc, yd, hssd, cbuf, sbuf = _ssd(proj, zeros_p['ssd'], zeros_p['cb'], zeros_p['sb'], *ssd_args,
                                        bp, tp, 256, 256)
        res = _out_ffn(hp, ya, yb, yc, yd, *ffn_args, 512, final)
        hp = res[0]
        if final:
            y_prompt = res[1]
        outs_p.append((
            jnp.transpose(kt.reshape(bp, HEADS, HEAD_DIM, tp), (0, 3, 1, 2)),
            jnp.transpose(vt.reshape(bp, HEADS, HEAD_DIM, tp), (0, 3, 1, 2)),
            logf[:, :HEADS].reshape(bp, tp, HEADS),
            jnp.stack([st5[:, 0, :S5_CH].reshape(bp, S5_G, S5_P),
                       st5[:, 0, S5_CH:].reshape(bp, S5_G, S5_P)], axis=-1),
            hssd.reshape(bp, HEADS, HEAD_DIM, SSD_STATE), cbuf, sbuf))

        projs = _in_proj(hs, g_mix, w_in_l, ns_tok)
        logfs, _ = _logf_cum(projs, fb, 1, ns_tok, ns_tok)
        proj_pad = jnp.pad(projs.reshape(bs, ts, PROJ_W), ((0, 0), (0, pad_t - ts), (0, 0)))
        proj_pad = proj_pad.reshape(bs * pad_t, PROJ_W)
        h0s5 = jnp.concatenate([state_s5[l][..., 0].reshape(bs, 1, S5_CH),
                                state_s5[l][..., 1].reshape(bs, 1, S5_CH)], axis=-1)
        ya, st5 = _s5(proj_pad, h0s5, *s5_args, bs, pad_t, pad_t, ts - 1)
        yc, yd, hssd, cbuf, sbuf = _ssd(proj_pad, state_ssd[l].reshape(bs, GROUP_W, SSD_STATE),
                                        state_ssd_conv[l], state_sconv[l], *ssd_args,
                                        bs, pad_t, pad_t, ts)
        unpad = lambda a: a.reshape(bs, pad_t, GROUP_W)[:, :ts].reshape(ns_tok, GROUP_W)

        qs = projs[:, COL_Q * GROUP_W:(COL_Q + 1) * GROUP_W].reshape(bs, ts, HEADS, HEAD_DIM)
        ks = projs[:, COL_K * GROUP_W:(COL_K + 1) * GROUP_W].reshape(bs, ts, HEADS, HEAD_DIM)
        vs = projs[:, COL_V * GROUP_W:(COL_V + 1) * GROUP_W].reshape(bs, ts, HEADS, HEAD_DIM)
        lfs = logfs[:, :HEADS].reshape(bs, ts, HEADS)
        qh = jnp.pad(jnp.swapaxes(qs * (FOX_SCALE * LOG2E), 1, 2), ((0, 0), (0, 0), (0, 8 - ts), (0, 0)))
        qbd = jnp.stack([jnp.pad(qh[:, h], ((0, 0), (0, 0), (h * HEAD_DIM, (HEADS - 1 - h) * HEAD_DIM)))
                         for h in range(HEADS)], axis=1).reshape(bs, QROWS, GROUP_W)
        as_page = lambda a: jnp.pad(jnp.transpose(a, (0, 2, 3, 1)).reshape(bs, GROUP_W, ts),
                                    ((0, 0), (0, 0), (0, PAGE - ts)))
        lnh = jnp.broadcast_to(jnp.swapaxes(lfs, 1, 2)[:, :, None, :], (bs, HEADS, 8, ts))
        lnh = jnp.pad(lnh.reshape(bs, QROWS, ts), ((0, 0), (0, 0), (0, 128 - ts)))
        rt = _past_bias(pt_flat, pool_t, l, bs, n_pages)
        oh = _fox_sample(pt_flat, ckt, cvt, qbd, as_page(ks), as_page(vs), rt, lnh, l, bs, n_pages, 16, ts)
        oh = oh.reshape(bs, HEADS, 8, HEADS, HEAD_DIM)[:, :, :ts]
        oh = jnp.stack([oh[:, h, :, h] for h in range(HEADS)], axis=2)
        yb = _rms_rows(oh.reshape(ns_tok, GROUP_W), fg)
        res = _out_ffn(hs, unpad(ya), yb, unpad(yc), unpad(yd), *ffn_args, ns_tok, final)
        hs = res[0]
        if final:
            y_sample = res[1]
        outs_s.append((
            ks, vs, lfs,
            jnp.stack([st5[:, 0, :S5_CH].reshape(bs, S5_G, S5_P),
                       st5[:, 0, S5_CH:].reshape(bs, S5_G, S5_P)], axis=-1),
            hssd.reshape(bs, HEADS, HEAD_DIM, SSD_STATE), cbuf, sbuf))

    stack = lambda outs: tuple(jnp.stack(col, axis=0) for col in zip(*outs))
    return ((y_prompt.reshape(bp, tp, D_MODEL), y_sample.reshape(bs, ts, D_MODEL))
            + stack(outs_p) + stack(outs_s))
```

```python
import functools
import math

import jax
import jax.numpy as jnp
from jax import lax
from jax.experimental import pallas as pl
from jax.experimental.pallas import tpu as pltpu

F32 = jnp.float32
BF16 = jnp.bfloat16

D_MODEL = 1024
DEPTH = 4
PAGE = 128
HEADS = 4
HEAD_DIM = 64
GROUP_W = 256
S5_G, S5_H, S5_P = 16, 16, 64
S5_CH = S5_G * S5_P
SSD_STATE = 128
SSD_CONV = 4
SC_CONV = 3
FFN = 4096
EPS = 1e-5
FOX_SCALE = HEAD_DIM ** -0.5
NEG = -1e30

COL_U, COL_Q, COL_K, COL_V, COL_Z, COL_XS, COL_B, COL_C, COL_SB, COL_SC, COL_SH = range(11)
MISC_OFF = 11 * GROUP_W
PROJ_W = MISC_OFF + 128
MISC_BLK = MISC_OFF // 128
DT_LANE = 4

VMEM_LIMIT = 56 * 1024 * 1024


def _cparams(sem):
    return pltpu.CompilerParams(dimension_semantics=sem, vmem_limit_bytes=VMEM_LIMIT)


def _const_spec(shape):
    nd = len(shape)
    return pl.BlockSpec(shape, lambda *_: (0,) * nd, pipeline_mode=pl.Buffered(1))


def _split3(x):
    hi = x.astype(BF16)
    r1 = x - hi.astype(F32)
    mid = r1.astype(BF16)
    lo = (r1 - mid.astype(F32)).astype(BF16)
    return hi, mid, lo


def _dot(a, b):
    return jnp.dot(a, b, preferred_element_type=F32)


def _dot_nt(a, b):
    return lax.dot_general(a, b, (((1,), (1,)), ((), ())), preferred_element_type=F32)


def _dot3_l(m_bf16, x):
    hi, mid, lo = _split3(x)
    return _dot(m_bf16, hi) + _dot(m_bf16, mid) + _dot(m_bf16, lo)


def _dot3_r(x, m_bf16):
    hi, mid, lo = _split3(x)
    return _dot(hi, m_bf16) + _dot(mid, m_bf16) + _dot(lo, m_bf16)


def _iota_mask(shape, fn):
    r = lax.broadcasted_iota(jnp.int32, shape, 0)
    c = lax.broadcasted_iota(jnp.int32, shape, 1)
    return fn(r, c)


def _tri_bf16(shape, fn):
    return jnp.where(_iota_mask(shape, fn), 1.0, 0.0).astype(BF16)


def _rms(x, g):
    return x * lax.rsqrt(jnp.mean(x * x, axis=-1, keepdims=True) + EPS) * g


def _sigmoid(x):
    return 1.0 / (1.0 + jnp.exp(-x))


def _softplus(x):
    return jnp.maximum(x, 0.0) + jnp.log1p(jnp.exp(-jnp.abs(x)))


def _lane_expand(cols, lane_head):
    out = jnp.where(lane_head == 2, cols[2], cols[3])
    out = jnp.where(lane_head == 1, cols[1], out)
    return jnp.where(lane_head == 0, cols[0], out)


def _in_proj_kernel(x_ref, g_ref, w_ref, o_ref):
    xn = _rms(x_ref[...], g_ref[...])
    o_ref[...] = _dot(xn.astype(BF16), w_ref[...])


def _in_proj(x, g, w, tm):
    n = x.shape[0]
    return pl.pallas_call(
        _in_proj_kernel,
        grid=(n // tm,),
        in_specs=[pl.BlockSpec((tm, D_MODEL), lambda i: (i, 0)),
                  _const_spec((1, D_MODEL)),
                  _const_spec((D_MODEL, PROJ_W))],
        out_specs=pl.BlockSpec((tm, PROJ_W), lambda i: (i, 0)),
        out_shape=jax.ShapeDtypeStruct((n, PROJ_W), F32),
        compiler_params=_cparams(("parallel",)),
        name="in_proj",
    )(x, g, w)


def _logf_cum_kernel(m_ref, b_ref, logf_ref, cum_ref, carry_ref, *, lc):
    @pl.when(pl.program_id(1) == 0)
    def _():
        carry_ref[...] = jnp.zeros_like(carry_ref)

    x = m_ref[...] + b_ref[...]
    logf = jnp.minimum(x, 0.0) - jnp.log1p(jnp.exp(-jnp.abs(x)))
    tri = _tri_bf16((lc, lc), lambda r, c: c <= r)
    cum = _dot3_l(tri, logf) + carry_ref[...]
    logf_ref[...] = logf
    cum_ref[...] = cum
    carry_ref[...] = cum[lc - 1:lc, :]


def _logf_cum(proj, bias, nseq, t, lc):
    nc = t // lc
    n = nseq * t
    return pl.pallas_call(
        functools.partial(_logf_cum_kernel, lc=lc),
        grid=(nseq, nc),
        in_specs=[pl.BlockSpec((lc, 128), lambda b, c: (b * nc + c, MISC_BLK)),
                  _const_spec((1, 128))],
        out_specs=[pl.BlockSpec((lc, 128), lambda b, c: (b * nc + c, 0)),
                   pl.BlockSpec((lc, 128), lambda b, c: (b * nc + c, 0))],
        out_shape=[jax.ShapeDtypeStruct((n, 128), F32), jax.ShapeDtypeStruct((n, 128), F32)],
        scratch_shapes=[pltpu.VMEM((1, 128), F32)],
        compiler_params=_cparams(("parallel", "arbitrary")),
        name="logf_cum",
    )(proj, bias)


AUG_W = HEADS * 128
LOG2E = math.log2(math.e)


def _bias_lanes(c, first):
    lane = lax.broadcasted_iota(jnp.int32, (1, AUG_W), 1)
    blk = lane // 128
    off = lane % 128 - HEAD_DIM
    hi, mid, lo = [t.astype(F32) for t in _split3(c)]
    out = jnp.where((off >= 3 - first) & (off < 6 - first), 1.0, 0.0)
    for j, t in enumerate((hi, mid, lo)):
        cols = [t[:, h:h + 1] for h in range(HEADS)]
        tl = jnp.where(blk == 2, cols[2], cols[3])
        tl = jnp.where(blk == 1, cols[1], tl)
        tl = jnp.where(blk == 0, cols[0], tl)
        out = jnp.where(off == first + j, tl, out)
    return out


def _fox_prep_kernel(m_ref, q_ref, k_ref, v_ref, b_ref, logf_ref, qa_ref, ka_ref, kt_ref, vt_ref,
                     vtb_ref, carry_ref, *, lc):
    @pl.when(pl.program_id(1) == 0)
    def _():
        carry_ref[...] = jnp.zeros_like(carry_ref)

    x = m_ref[...] + b_ref[...]
    logf = jnp.minimum(x, 0.0) - jnp.log1p(jnp.exp(-jnp.abs(x)))
    tri = _tri_bf16((lc, lc), lambda r, c: c <= r)
    cum = _dot3_l(tri, logf) + carry_ref[...]
    logf_ref[...] = logf
    carry_ref[...] = cum[lc - 1:lc, :]

    place = _tri_bf16((GROUP_W, AUG_W),
                      lambda r, c: (c // 128 == r // HEAD_DIM) & (c % 128 == r % HEAD_DIM))
    c2 = cum * LOG2E
    q = (q_ref[...] * (FOX_SCALE * LOG2E)).astype(BF16)
    qa_ref[...] = (_dot(q, place) + _bias_lanes(c2, 0)).astype(BF16)
    k = k_ref[...]
    ka_ref[...] = (_dot(k.astype(BF16), place) + _bias_lanes(-c2, 3)).astype(BF16)
    kt_ref[...] = jnp.transpose(k)
    vt = jnp.transpose(v_ref[...])
    vt_ref[...] = vt
    vtb_ref[...] = vt.astype(BF16)


def _fox_prep(proj, bias, nseq, t, lc):
    nc = t // lc
    n = nseq * t
    rows = lambda w, cb: pl.BlockSpec((lc, w), lambda b, c: (b * nc + c, cb))
    chan = pl.BlockSpec((None, GROUP_W, lc), lambda b, c: (b, 0, c))
    return pl.pallas_call(
        functools.partial(_fox_prep_kernel, lc=lc),
        grid=(nseq, nc),
        in_specs=[rows(128, MISC_BLK), rows(GROUP_W, COL_Q), rows(GROUP_W, COL_K), rows(GROUP_W, COL_V),
                  _const_spec((1, 128))],
        out_specs=[rows(128, 0), rows(AUG_W, 0), rows(AUG_W, 0), chan, chan, chan],
        out_shape=[jax.ShapeDtypeStruct((n, 128), F32), jax.ShapeDtypeStruct((n, AUG_W), BF16),
                   jax.ShapeDtypeStruct((n, AUG_W), BF16),
                   jax.ShapeDtypeStruct((nseq, GROUP_W, t), F32),
                   jax.ShapeDtypeStruct((nseq, GROUP_W, t), F32),
                   jax.ShapeDtypeStruct((nseq, GROUP_W, t), BF16)],
        scratch_shapes=[pltpu.VMEM((1, 128), F32)],
        compiler_params=_cparams(("parallel", "arbitrary")),
        name="fox_prep",
    )(proj, proj, proj, proj, bias)


def _s5_kernel(u_ref, h0_ref, bri_ref, ak_ref, ct_ref, d_ref, wg_ref, g_ref,
               y_ref, st_ref, bu_ref, xc_ref, *, tc, state_row):
    c = pl.program_id(1)

    @pl.when(c == 0)
    def _():
        h0 = h0_ref[...]
        xc_ref[0] = jnp.broadcast_to(h0[:, :S5_CH], (8, S5_CH))
        xc_ref[1] = jnp.broadcast_to(h0[:, S5_CH:], (8, S5_CH))

    u = u_ref[...]
    bu_ref[...] = _dot(u.astype(BF16), bri_ref[...])

    def tile(i, carry):
        xr_c, xi_c = carry
        r0 = pl.multiple_of(i * 8, 8)
        dr = bu_ref[pl.ds(r0, 8), 0:S5_CH]
        di = bu_ref[pl.ds(r0, 8), S5_CH:2 * S5_CH]
        for j, k in enumerate((1, 2, 4)):
            ar = ak_ref[2 + 2 * j]
            ai = ak_ref[3 + 2 * j]
            sr = pltpu.roll(dr, k, axis=0)
            si = pltpu.roll(di, k, axis=0)
            dr, di = dr + ar * sr - ai * si, di + ar * si + ai * sr
        pr = ak_ref[0]
        pi = ak_ref[1]
        xr = dr + pr * xr_c - pi * xi_c
        xi = di + pr * xi_c + pi * xr_c
        bu_ref[pl.ds(r0, 8), 0:S5_CH] = xr
        bu_ref[pl.ds(r0, 8), S5_CH:2 * S5_CH] = xi
        return (jnp.broadcast_to(xr[7:8, :], (8, S5_CH)),
                jnp.broadcast_to(xi[7:8, :], (8, S5_CH)))

    xr_c, xi_c = lax.fori_loop(0, tc // 8, tile, (xc_ref[0], xc_ref[1]))
    xc_ref[0] = xr_c
    xc_ref[1] = xi_c

    @pl.when(c == pl.num_programs(1) - 1)
    def _():
        st_ref[...] = bu_ref[state_row:state_row + 1, :]

    y = _dot(bu_ref[...].astype(BF16), ct_ref[...]) + d_ref[...] * u
    y = 0.5 * y * (1.0 + jnp.tanh(math.sqrt(2.0 / math.pi) * (y + 0.044715 * (y * y * y))))
    y = y * _sigmoid(_dot(y.astype(BF16), wg_ref[...]))
    y_ref[...] = _rms(y, g_ref[...])


def _s5(proj, h0, bri, ak, ct, d, wg, g, nseq, t, tc, state_row):
    nc = t // tc
    n = nseq * t
    return pl.pallas_call(
        functools.partial(_s5_kernel, tc=tc, state_row=state_row),
        grid=(nseq, nc),
        in_specs=[pl.BlockSpec((tc, GROUP_W), lambda b, c: (b * nc + c, COL_U)),
                  pl.BlockSpec((None, 1, 2 * S5_CH), lambda b, c: (b, 0, 0)),
                  _const_spec((GROUP_W, 2 * S5_CH)),
                  _const_spec((8, 8, S5_CH)),
                  _const_spec((2 * S5_CH, GROUP_W)),
                  _const_spec((1, GROUP_W)),
                  _const_spec((GROUP_W, GROUP_W)),
                  _const_spec((1, GROUP_W))],
        out_specs=[pl.BlockSpec((tc, GROUP_W), lambda b, c: (b * nc + c, 0)),
                   pl.BlockSpec((None, 1, 2 * S5_CH), lambda b, c: (b, 0, 0))],
        out_shape=[jax.ShapeDtypeStruct((n, GROUP_W), F32),
                   jax.ShapeDtypeStruct((nseq, 1, 2 * S5_CH), F32)],
        scratch_shapes=[pltpu.VMEM((tc, 2 * S5_CH), F32), pltpu.VMEM((2, 8, S5_CH), F32)],
        compiler_params=_cparams(("parallel", "arbitrary")),
        name="s5",
    )(proj, h0, bri, ak, ct, d, wg, g)


def _fox_prompt_kernel(qi_ref, ki_ref, q_ref, k_ref, vt_ref, g_ref, o_ref,
                       m_ref, l_ref, acc_ref, s_ref, *, tq, tk):
    step = pl.program_id(1)
    qi = qi_ref[step]
    ki = ki_ref[step]

    @pl.when(ki == 0)
    def _():
        m_ref[...] = jnp.full_like(m_ref, NEG)
        l_ref[...] = jnp.zeros_like(l_ref)
        acc_ref[...] = jnp.zeros_like(acc_ref)

    def update(masked):
        if masked:
            causal = _iota_mask((tk, tq), lambda r, c: r <= c)
        for h in range(HEADS):
            s_ref[h] = _dot_nt(k_ref[:, h * 128:(h + 1) * 128], q_ref[:, h * 128:(h + 1) * 128])
        for h in range(HEADS):
            s = s_ref[h]
            if masked:
                s = jnp.where(causal, s, NEG)
            m_old = m_ref[h]
            m_new = jnp.maximum(m_old, jnp.max(s, axis=0, keepdims=True))
            alpha = jnp.exp2(m_old - m_new)
            p = jnp.exp2(s - m_new)
            l_ref[h] = alpha * l_ref[h] + jnp.sum(p, axis=0, keepdims=True)
            vth = vt_ref[h * HEAD_DIM:(h + 1) * HEAD_DIM, :]
            acc_ref[h] = alpha * acc_ref[h] + _dot(vth, p.astype(BF16))
            m_ref[h] = m_new

    @pl.when(ki < qi)
    def _():
        update(False)

    @pl.when(ki == qi)
    def _():
        update(True)
        parts = [acc_ref[h] / l_ref[h] for h in range(HEADS)]
        ot = jnp.concatenate(parts, axis=0)
        ot = ot * lax.rsqrt(jnp.mean(ot * ot, axis=0, keepdims=True) + EPS) * g_ref[...]
        o_ref[...] = jnp.transpose(ot)


def _fox_prompt(qa, ka, vtb, g, nseq, t, tq):
    tk = tq
    nq = t // tq
    n = nseq * t
    pairs = [(i, j) for i in range(nq) for j in range(i + 1)]
    qi_tab = jnp.asarray([p[0] for p in pairs], jnp.int32)
    ki_tab = jnp.asarray([p[1] for p in pairs], jnp.int32)
    grid_spec = pltpu.PrefetchScalarGridSpec(
        num_scalar_prefetch=2,
        grid=(nseq, len(pairs)),
        in_specs=[pl.BlockSpec((tq, AUG_W), lambda b, s, qt, kt: (b * nq + qt[s], 0)),
                  pl.BlockSpec((tk, AUG_W), lambda b, s, qt, kt: (b * nq + kt[s], 0)),
                  pl.BlockSpec((None, GROUP_W, tk), lambda b, s, qt, kt: (b, 0, kt[s])),
                  pl.BlockSpec((GROUP_W, 1), lambda b, s, qt, kt: (0, 0))],
        out_specs=pl.BlockSpec((tq, GROUP_W), lambda b, s, qt, kt: (b * nq + qt[s], 0)),
        scratch_shapes=[pltpu.VMEM((HEADS, 1, tq), F32),
                        pltpu.VMEM((HEADS, 1, tq), F32),
                        pltpu.VMEM((HEADS, HEAD_DIM, tq), F32),
                        pltpu.VMEM((HEADS, tk, tq), F32)],
    )
    return pl.pallas_call(
        functools.partial(_fox_prompt_kernel, tq=tq, tk=tk),
        grid_spec=grid_spec,
        out_shape=jax.ShapeDtypeStruct((n, GROUP_W), F32),
        compiler_params=_cparams(("parallel", "arbitrary")),
        name="fox_prompt",
    )(qi_tab, ki_tab, qa, ka, vtb, g)


PAGE_ROWS = PAGE * HEADS


def _past_bias_kernel(pt_ref, pool_ref, o_ref, lbuf_ref, sem, *, layer, n_pages):
    b = pl.program_id(0)

    def page_copy(p):
        return pltpu.make_async_copy(pool_ref.at[layer, pt_ref[b * n_pages + p]],
                                     lbuf_ref.at[pl.ds(p * 8, 8)], sem.at[0])

    def start(p, _):
        page_copy(p).start()
        return 0

    def wait(p, _):
        page_copy(p).wait()
        return 0

    lax.fori_loop(0, n_pages, start, 0)
    lax.fori_loop(0, n_pages, wait, 0)

    x = lbuf_ref[...]
    rows = n_pages * 8
    after = _tri_bf16((PAGE, PAGE), lambda r, c: r > c)
    local = _dot3_r(x, after)
    ones = jnp.ones((PAGE, 128), BF16)
    tot = _dot3_r(x, ones)
    later = _tri_bf16((rows, rows), lambda r, c: (c % 8 == r % 8) & (c // 8 > r // 8))
    res = (local + _dot3_l(later, tot)) * LOG2E
    for p in range(n_pages):
        o_ref[:, p * PAGE:(p + 1) * PAGE] = res[p * 8:(p + 1) * 8, :]


def _past_bias(page_table_flat, pool_t, layer, nseq, n_pages):
    rows = n_pages * 8
    grid_spec = pltpu.PrefetchScalarGridSpec(
        num_scalar_prefetch=1,
        grid=(nseq,),
        in_specs=[pl.BlockSpec(memory_space=pl.ANY)],
        out_specs=pl.BlockSpec((None, 8, n_pages * PAGE), lambda b, pt: (b, 0, 0)),
        scratch_shapes=[pltpu.VMEM((rows, 128), F32), pltpu.SemaphoreType.DMA((1,))],
    )
    return pl.pallas_call(
        functools.partial(_past_bias_kernel, layer=layer, n_pages=n_pages),
        grid_spec=grid_spec,
        out_shape=jax.ShapeDtypeStruct((nseq, 8, n_pages * PAGE), F32),
        compiler_params=_cparams(("arbitrary",)),
        name="past_bias",
    )(page_table_flat, pool_t)


def _fox_sample_kernel(pt_ref, ck_ref, cv_ref, q_ref, kn_ref, vn_ref, rt_ref, ln_ref,
                       o_ref, kbuf, vbuf, sem, m_ref, l_ref, acc_ref,
                       *, layer, n_pages, pg, dec_t):
    g = pl.program_id(0)
    ng = pl.num_programs(0)
    nc = n_pages // pg
    c = g % nc
    slot = g % 2

    def copies(step, sl):
        out = []
        for j in range(pg):
            pid = pt_ref[step * pg + j]
            out.append(pltpu.make_async_copy(ck_ref.at[layer, pid], kbuf.at[sl, j], sem.at[0, sl]))
            out.append(pltpu.make_async_copy(cv_ref.at[layer, pid], vbuf.at[sl, j], sem.at[1, sl]))
        return out

    @pl.when(g == 0)
    def _():
        for cp in copies(0, 0):
            cp.start()

    @pl.when(g + 1 < ng)
    def _():
        for cp in copies(g + 1, 1 - slot):
            cp.start()

    for cp in copies(g, slot):
        cp.wait()

    row_tok = lax.broadcasted_iota(jnp.int32, (QROWS, 1), 0) % 8

    @pl.when(c == 0)
    def _():
        m_ref[...] = jnp.full_like(m_ref, NEG)
        l_ref[...] = jnp.zeros_like(l_ref)
        acc_ref[...] = jnp.zeros_like(acc_ref)

    ln = ln_ref[...] * LOG2E
    cums = [ln[:, 0:1]]
    for t in range(1, dec_t):
        cums.append(cums[-1] + ln[:, t:t + 1])
    pcol = cums[dec_t - 1]
    for t in range(dec_t - 2, -1, -1):
        pcol = jnp.where(row_tok == t, cums[t], pcol)

    qbd = q_ref[...].astype(BF16)

    def attend(s, vts):
        m_old = m_ref[...]
        m_new = jnp.maximum(m_old, jnp.max(s, axis=1, keepdims=True))
        alpha = jnp.exp2(m_old - m_new)
        p = jnp.exp2(s - m_new)
        l_ref[...] = alpha * l_ref[...] + jnp.sum(p, axis=1, keepdims=True)
        pb = p.astype(BF16)
        pv = [_dot_nt(pb[:, j * PAGE:(j + 1) * PAGE], vt) for j, vt in enumerate(vts)]
        while len(pv) > 1:
            pv = [a + b for a, b in zip(pv[0::2], pv[1::2])]
        acc_ref[...] = alpha * acc_ref[...] + pv[0]
        m_ref[...] = m_new

    rt = rt_ref[...]
    bias = jnp.concatenate([jnp.broadcast_to(rt[h:h + 1, :], (8, pg * PAGE)) for h in range(HEADS)], axis=0)
    s = jnp.concatenate([_dot(qbd, kbuf[slot, j].astype(BF16)) for j in range(pg)], axis=1)
    attend(s + bias + pcol, [vbuf[slot, j].astype(BF16) for j in range(pg)])

    @pl.when(c == nc - 1)
    def _():
        lane = lax.broadcasted_iota(jnp.int32, (1, PAGE), 1)
        pt_l = jnp.broadcast_to(cums[dec_t - 1], (QROWS, PAGE))
        for t in range(dec_t - 2, -1, -1):
            pt_l = jnp.where(lane == t, cums[t], pt_l)
        s_new = _dot(qbd, kn_ref[...].astype(BF16)) + pcol - pt_l
        s_new = jnp.where((lane < dec_t) & (lane <= row_tok), s_new, NEG)
        attend(s_new, [vn_ref[...].astype(BF16)])
        o_ref[...] = acc_ref[...] / l_ref[...]


QROWS = HEADS * 8


def _fox_sample(pt_flat, ckt, cvt, qbd, knt, vnt, rt, ln, layer, nseq, n_pages, pg, dec_t):
    nc = n_pages // pg
    hd = HEADS * HEAD_DIM
    per_seq = lambda shape: pl.BlockSpec((None,) + shape, lambda g, pt: (g // nc,) + (0,) * len(shape))
    grid_spec = pltpu.PrefetchScalarGridSpec(
        num_scalar_prefetch=1,
        grid=(nseq * nc,),
        in_specs=[pl.BlockSpec(memory_space=pl.ANY),
                  pl.BlockSpec(memory_space=pl.ANY),
                  per_seq((QROWS, hd)),
                  per_seq((hd, PAGE)),
                  per_seq((hd, PAGE)),
                  pl.BlockSpec((None, 8, pg * PAGE), lambda g, pt: (g // nc, 0, g % nc)),
                  per_seq((QROWS, 128))],
        out_specs=per_seq((QROWS, hd)),
        scratch_shapes=[pltpu.VMEM((2, pg, hd, PAGE), F32),
                        pltpu.VMEM((2, pg, hd, PAGE), F32),
                        pltpu.SemaphoreType.DMA((2, 2)),
                        pltpu.VMEM((QROWS, 1), F32),
                        pltpu.VMEM((QROWS, 1), F32),
                        pltpu.VMEM((QROWS, hd), F32)],
    )
    return pl.pallas_call(
        functools.partial(_fox_sample_kernel, layer=layer, n_pages=n_pages, pg=pg, dec_t=dec_t),
        grid_spec=grid_spec,
        out_shape=jax.ShapeDtypeStruct((nseq, QROWS, hd), F32),
        compiler_params=_cparams(("arbitrary",)),
        name="fox_sample",
    )(pt_flat, ckt, cvt, qbd, knt, vnt, rt, ln)


def _rms_rows_kernel(x_ref, g_ref, o_ref):
    o_ref[...] = _rms(x_ref[...], g_ref[...])


def _rms_rows(x, g):
    return pl.pallas_call(
        _rms_rows_kernel,
        out_shape=jax.ShapeDtypeStruct(x.shape, F32),
        name="rms_rows",
    )(x, g)


def _ssd_kernel(z_ref, xs_ref, b_ref, c_ref, sb_ref, sc_ref, sh_ref, misc_ref,
                h0_ref, cb0_ref, sb0_ref,
                cw_ref, cbias_ref, dtb_ref, alog_ref, dskip_ref, ng_ref, scw_ref, scg_ref,
                yc_ref, yd_ref, hout_ref, cbout_ref, sbout_ref,
                ext_ref, exts_ref, h_ref, *, L, valid):
    c = pl.program_id(1)
    last = pl.num_programs(1) - 1
    XW = 3 * GROUP_W

    @pl.when(c == 0)
    def _():
        h_ref[...] = h0_ref[...]
        ext_ref[8 - (SSD_CONV - 1):8, :] = cb0_ref[...]
        exts_ref[8 - (SC_CONV - 1):8, :] = sb0_ref[...]

    ext_ref[8:8 + L, 0:GROUP_W] = xs_ref[...]
    ext_ref[8:8 + L, GROUP_W:2 * GROUP_W] = b_ref[...]
    ext_ref[8:8 + L, 2 * GROUP_W:XW] = c_ref[...]
    exts_ref[8:8 + L, :] = sc_ref[...] * sh_ref[...]

    cw = cw_ref[...]
    acc = cbias_ref[...] + cw[0:1, :] * ext_ref[5:5 + L, :]
    for j in range(1, SSD_CONV):
        acc = acc + cw[j:j + 1, :] * ext_ref[5 + j:5 + j + L, :]
    xbc = acc * _sigmoid(acc)
    xs = xbc[:, 0:GROUP_W]
    bm = xbc[:, GROUP_W:2 * GROUP_W].astype(BF16)
    cm = xbc[:, 2 * GROUP_W:XW].astype(BF16)

    scw = scw_ref[...]
    conv = scw[0:1, :] * exts_ref[6:6 + L, :]
    for j in range(1, SC_CONV):
        conv = conv + scw[j:j + 1, :] * exts_ref[6 + j:6 + j + L, :]
    yd_ref[...] = _rms(sb_ref[...] * conv, scg_ref[...])

    @pl.when(c == last)
    def _():
        cbout_ref[...] = ext_ref[8 + valid - (SSD_CONV - 1):8 + valid, :]
        sbout_ref[...] = exts_ref[8 + valid - (SC_CONV - 1):8 + valid, :]

    ext_ref[8 - (SSD_CONV - 1):8, :] = ext_ref[8 + L - (SSD_CONV - 1):8 + L, :]
    exts_ref[8 - (SC_CONV - 1):8, :] = exts_ref[8 + L - (SC_CONV - 1):8 + L, :]

    lane128 = lax.broadcasted_iota(jnp.int32, (1, 128), 1)
    rowi = lax.broadcasted_iota(jnp.int32, (L, 1), 0)
    live = (lane128 >= DT_LANE) & (lane128 < DT_LANE + HEADS) & (rowi < valid)
    dt = jnp.where(live, _softplus(misc_ref[...] + dtb_ref[...]), 0.0)
    a = dt * (-jnp.exp(alog_ref[...]))
    incl = _tri_bf16((L, L), lambda r, cc: cc <= r)
    after = _tri_bf16((L, L), lambda r, cc: cc > r)
    acum = _dot3_l(incl, a)
    rev = _dot3_l(after, a)
    lower = _iota_mask((L, L), lambda r, cc: cc <= r)
    upto = _tri_bf16((L, L), lambda r, cc: r <= cc)
    acum_t = _dot3_r(jnp.transpose(a)[0:8, :], upto)

    lane_head = lax.broadcasted_iota(jnp.int32, (1, GROUP_W), 1) // HEAD_DIM
    col = lambda arr, h: arr[:, DT_LANE + h:DT_LANE + h + 1]
    dt_l = _lane_expand([col(dt, h) for h in range(HEADS)], lane_head)
    eac_l = _lane_expand([jnp.exp(col(acum, h)) for h in range(HEADS)], lane_head)
    erev_l = _lane_expand([jnp.exp(col(rev, h)) for h in range(HEADS)], lane_head)
    xdt = xs * dt_l
    xdt_b = xdt.astype(BF16)

    gmat = [_dot_nt(cm[:, g * SSD_STATE:(g + 1) * SSD_STATE], bm[:, g * SSD_STATE:(g + 1) * SSD_STATE])
            for g in range(2)]
    y = jnp.zeros((L, GROUP_W), F32)
    for h in range(HEADS):
        delta = col(acum, h) - acum_t[DT_LANE + h:DT_LANE + h + 1, :]
        lmat = jnp.exp(jnp.where(lower, delta, NEG))
        scores = (gmat[h // 2] * lmat).astype(BF16)
        y = jnp.where(lane_head == h, _dot(scores, xdt_b), y)

    h_all = h_ref[...]
    hb = h_all.astype(BF16)
    y0 = _dot_nt(cm[:, 0:SSD_STATE], hb)
    y1 = _dot_nt(cm[:, SSD_STATE:2 * SSD_STATE], hb)
    y = y + jnp.concatenate([y0[:, 0:128], y1[:, 128:256]], axis=1) * eac_l

    wt = jnp.transpose(xdt * erev_l).astype(BF16)
    s0 = _dot(wt, bm[:, 0:SSD_STATE])
    s1 = _dot(wt, bm[:, SSD_STATE:2 * SSD_STATE])
    upd = jnp.concatenate([s0[0:128, :], s1[128:256, :]], axis=0)
    row_head = lax.broadcasted_iota(jnp.int32, (GROUP_W, 1), 0) // HEAD_DIM
    dec = [jnp.exp(acum[L - 1:L, DT_LANE + h:DT_LANE + h + 1]) for h in range(HEADS)]
    scale = jnp.where(row_head == 2, dec[2], dec[3])
    scale = jnp.where(row_head == 1, dec[1], scale)
    scale = jnp.where(row_head == 0, dec[0], scale)
    h_new = scale * h_all + upd
    h_ref[...] = h_new

    @pl.when(c == last)
    def _():
        hout_ref[...] = h_new

    y = y + dskip_ref[...] * xs
    zz = z_ref[...]
    y = y * (zz * _sigmoid(zz))
    yc_ref[...] = _rms(y, ng_ref[...])


def _ssd(proj, h0, cb0, sb0, cw, cbias, dtb, alog, dskip, ng, scw, scg, nseq, t, L, valid):
    nc = t // L
    n = nseq * t
    XW = 3 * GROUP_W

    def col_spec(cb):
        return pl.BlockSpec((L, GROUP_W), lambda b, c: (b * nc + c, cb))

    state_specs = [pl.BlockSpec((None, GROUP_W, SSD_STATE), lambda b, c: (b, 0, 0)),
                   pl.BlockSpec((None, SSD_CONV - 1, XW), lambda b, c: (b, 0, 0)),
                   pl.BlockSpec((None, SC_CONV - 1, GROUP_W), lambda b, c: (b, 0, 0))]
    return pl.pallas_call(
        functools.partial(_ssd_kernel, L=L, valid=valid),
        grid=(nseq, nc),
        in_specs=[col_spec(COL_Z), col_spec(COL_XS), col_spec(COL_B), col_spec(COL_C),
                  col_spec(COL_SB), col_spec(COL_SC), col_spec(COL_SH),
                  pl.BlockSpec((L, 128), lambda b, c: (b * nc + c, MISC_BLK))]
                 + state_specs
                 + [_const_spec((SSD_CONV, XW)), _const_spec((1, XW)), _const_spec((1, 128)),
                    _const_spec((1, 128)), _const_spec((1, GROUP_W)), _const_spec((1, GROUP_W)),
                    _const_spec((SC_CONV, GROUP_W)), _const_spec((1, GROUP_W))],
        out_specs=[pl.BlockSpec((L, GROUP_W), lambda b, c: (b * nc + c, 0)),
                   pl.BlockSpec((L, GROUP_W), lambda b, c: (b * nc + c, 0))] + state_specs,
        out_shape=[jax.ShapeDtypeStruct((n, GROUP_W), F32),
                   jax.ShapeDtypeStruct((n, GROUP_W), F32),
                   jax.ShapeDtypeStruct((nseq, GROUP_W, SSD_STATE), F32),
                   jax.ShapeDtypeStruct((nseq, SSD_CONV - 1, XW), F32),
                   jax.ShapeDtypeStruct((nseq, SC_CONV - 1, GROUP_W), F32)],
        scratch_shapes=[pltpu.VMEM((8 + L, XW), F32), pltpu.VMEM((8 + L, GROUP_W), F32),
                        pltpu.VMEM((GROUP_W, SSD_STATE), F32)],
        compiler_params=_cparams(("parallel", "arbitrary")),
        name="ssd",
    )(proj, proj, proj, proj, proj, proj, proj, proj, h0, cb0, sb0,
      cw, cbias, dtb, alog, dskip, ng, scw, scg)


def _out_ffn_kernel(x_ref, ya_ref, yb_ref, yc_ref, yd_ref, wo_ref, g_ref, wu_ref, wd_ref, gf_ref,
                    *out_refs, final):
    x1 = x_ref[...]
    for i, y_ref in enumerate((ya_ref, yb_ref, yc_ref, yd_ref)):
        x1 = x1 + _dot(y_ref[...].astype(BF16), wo_ref[i * GROUP_W:(i + 1) * GROUP_W, :])
    hn = _rms(x1, g_ref[...]).astype(BF16)
    fc = 1024
    parts = []
    for j in range(FFN // fc):
        hj = jnp.maximum(_dot(hn, wu_ref[:, j * fc:(j + 1) * fc]), 0.0)
        parts.append(_dot((hj * hj).astype(BF16), wd_ref[j * fc:(j + 1) * fc, :]))
    acc = x1 + ((parts[0] + parts[1]) + (parts[2] + parts[3]))
    out_refs[0][...] = acc
    if final:
        out_refs[1][...] = _rms(acc, gf_ref[...])


def _out_ffn(x, ya, yb, yc, yd, wo, g, wu, wd, gf, tm, final):
    n = x.shape[0]
    row = lambda w: pl.BlockSpec((tm, w), lambda i: (i, 0))
    n_out = 2 if final else 1
    return pl.pallas_call(
        functools.partial(_out_ffn_kernel, final=final),
        grid=(n // tm,),
        in_specs=[row(D_MODEL), row(GROUP_W), row(GROUP_W), row(GROUP_W), row(GROUP_W),
                  _const_spec((D_MODEL, D_MODEL)), _const_spec((1, D_MODEL)),
                  _const_spec((D_MODEL, FFN)), _const_spec((FFN, D_MODEL)),
                  _const_spec((1, D_MODEL))],
        out_specs=[row(D_MODEL)] * n_out,
        out_shape=[jax.ShapeDtypeStruct((n, D_MODEL), F32)] * n_out,
        compiler_params=_cparams(("parallel",)),
        name="out_ffn",
    )(x, ya, yb, yc, yd, wo, g, wu, wd, gf)


def _s5_params(lam_re, lam_im, log_dt, b_re, b_im, c_re, c_im):
    dt = jnp.exp(log_dt)[:, None]
    mag = jnp.exp(lam_re * dt)
    abr, abi = mag * jnp.cos(lam_im * dt), mag * jnp.sin(lam_im * dt)
    den = lam_re * lam_re + lam_im * lam_im
    qr = ((abr - 1.0) * lam_re + abi * lam_im) / den
    qi = (abi * lam_re - (abr - 1.0) * lam_im) / den
    bbr = qr[..., None] * b_re - qi[..., None] * b_im
    bbi = qr[..., None] * b_im + qi[..., None] * b_re
    eye = jnp.eye(S5_G, dtype=F32)
    bd = lambda m: jnp.einsum('gph,gk->ghkp', m, eye).reshape(S5_G * S5_H, S5_CH)
    bri = jnp.concatenate([bd(bbr), bd(bbi)], axis=1).astype(BF16)
    cd = lambda m: jnp.einsum('ghp,gk->gpkh', m, eye).reshape(S5_CH, S5_G * S5_H)
    ct = jnp.concatenate([cd(c_re), -cd(c_im)], axis=0).astype(BF16)
    ar, ai = abr.reshape(1, S5_CH), abi.reshape(1, S5_CH)
    pows = [(ar, ai)]
    for _ in range(7):
        pr, pi = pows[-1]
        pows.append((pr * ar - pi * ai, pr * ai + pi * ar))
    rows = jnp.arange(8)[:, None]
    tabs = [jnp.concatenate([p[0] for p in pows], axis=0), jnp.concatenate([p[1] for p in pows], axis=0)]
    for k in (1, 2, 4):
        for part in pows[k - 1]:
            tabs.append(jnp.where(rows >= k, part, 0.0))
    return bri, jnp.stack(tabs, axis=0), ct


def _pad_lanes(v, off):
    return jnp.zeros((1, 128), F32).at[0, off:off + v.shape[0]].set(v)


def _permute_w_in(w):
    return jnp.concatenate(
        [w[:, 0:1024], w[:, 1028:2052], w[:, 2056:2824], w[:, 1024:1028], w[:, 2052:2056],
         jnp.zeros((w.shape[0], 120), w.dtype)], axis=1).astype(BF16)


def kernel(x_prompt, x_sample, cache_k, cache_v, cache_logf, state_s5, state_ssd, state_ssd_conv, state_sconv, page_table, norm_mix_g, w_in, s5_lam_re, s5_lam_im, s5_log_dt, s5_b_re, s5_b_im, s5_c_re, s5_c_im, s5_d, s5_w_glu, s5_norm_g, fox_b_f, fox_norm_g, ssd_conv_w, ssd_conv_b, ssd_dt_bias, ssd_a_log, ssd_d, ssd_norm_g, sc_conv_w, sc_norm_g, w_out, norm_ffn_g, w_up, w_down, norm_final_g):
    bp, tp, _ = x_prompt.shape
    bs, ts, _ = x_sample.shape
    n_pool = cache_k.shape[1]
    n_pages = page_table.shape[1]
    np_tok, ns_tok = bp * tp, bs * ts
    pad_t = 32
    S5_PAD = 8

    pt_flat = page_table.reshape(-1)
    ckt = jnp.transpose(cache_k, (0, 1, 3, 4, 2)).reshape(DEPTH, n_pool, GROUP_W, PAGE)
    cvt = jnp.transpose(cache_v, (0, 1, 3, 4, 2)).reshape(DEPTH, n_pool, GROUP_W, PAGE)
    pool_t = jnp.pad(jnp.swapaxes(cache_logf, 2, 3), ((0, 0), (0, 0), (0, 8 - HEADS), (0, 0)))
    gfin = norm_final_g.reshape(1, D_MODEL)

    hp = x_prompt.reshape(np_tok, D_MODEL)
    hs = x_sample.reshape(ns_tok, D_MODEL)
    zeros_p = dict(
        s5=jnp.zeros((bp, 1, 2 * S5_CH), F32), ssd=jnp.zeros((bp, GROUP_W, SSD_STATE), F32),
        cb=jnp.zeros((bp, SSD_CONV - 1, 3 * GROUP_W), F32), sb=jnp.zeros((bp, SC_CONV - 1, GROUP_W), F32))
    outs_p, outs_s = [], []
    y_prompt = y_sample = None

    for l in range(DEPTH):
        final = l == DEPTH - 1
        w_in_l = _permute_w_in(w_in[l])
        g_mix = norm_mix_g[l].reshape(1, D_MODEL)
        bri, ak, ct = _s5_params(s5_lam_re[l], s5_lam_im[l], s5_log_dt[l], s5_b_re[l], s5_b_im[l],
                                 s5_c_re[l], s5_c_im[l])
        s5_args = (bri, ak, ct, s5_d[l].reshape(1, GROUP_W), s5_w_glu[l].astype(BF16),
                   s5_norm_g[l].reshape(1, GROUP_W))
        fb = _pad_lanes(fox_b_f[l], 0)
        fg = fox_norm_g[l].reshape(1, GROUP_W)
        ssd_args = (ssd_conv_w[l], ssd_conv_b[l].reshape(1, -1), _pad_lanes(ssd_dt_bias[l], DT_LANE),
                    _pad_lanes(ssd_a_log[l], DT_LANE), jnp.repeat(ssd_d[l], HEAD_DIM).reshape(1, GROUP_W),
                    ssd_norm_g[l].reshape(1, GROUP_W), sc_conv_w[l], sc_norm_g[l].reshape(1, GROUP_W))
        ffn_args = (w_out[l].astype(BF16), norm_ffn_g[l].reshape(1, D_MODEL), w_up[l].astype(BF16),
                    w_down[l].astype(BF16), gfin)

        proj = _in_proj(hp, g_mix, w_in_l, 512)
        logf, qa, ka, kt, vt, vtb = _fox_prep(proj, fb, bp, tp, 512)
        ya, st5 = _s5(proj, zeros_p['s5'], *s5_args, bp, tp, 512, 511)
        yb = _fox_prompt(qa, ka, vtb, fox_norm_g[l].reshape(GROUP_W, 1), bp, tp, 512)
        yc, yd, hssd, cbuf, sbuf = _ssd(proj, zeros_p['ssd'], zeros_p['cb'], zeros_p['sb'], *ssd_args,
                                        bp, tp, 256, 256)
        res = _out_ffn(hp, ya, yb, yc, yd, *ffn_args, 512, final)
        hp = res[0]
        if final:
            y_prompt = res[1]
        outs_p.append((
            jnp.transpose(kt.reshape(bp, HEADS, HEAD_DIM, tp), (0, 3, 1, 2)),
            jnp.transpose(vt.reshape(bp, HEADS, HEAD_DIM, tp), (0, 3, 1, 2)),
            logf[:, :HEADS].reshape(bp, tp, HEADS),
            jnp.stack([st5[:, 0, :S5_CH].reshape(bp, S5_G, S5_P),
                       st5[:, 0, S5_CH:].reshape(bp, S5_G, S5_P)], axis=-1),
            hssd.reshape(bp, HEADS, HEAD_DIM, SSD_STATE), cbuf, sbuf))

        projs = _in_proj(hs, g_mix, w_in_l, ns_tok)
        logfs, _ = _logf_cum(projs, fb, 1, ns_tok, ns_tok)
        padded = lambda r: jnp.pad(projs.reshape(bs, ts, PROJ_W),
                                   ((0, 0), (0, r - ts), (0, 0))).reshape(bs * r, PROJ_W)
        h0s5 = jnp.concatenate([state_s5[l][..., 0].reshape(bs, 1, S5_CH),
                                state_s5[l][..., 1].reshape(bs, 1, S5_CH)], axis=-1)
        ya, st5 = _s5(padded(S5_PAD), h0s5, *s5_args, bs, S5_PAD, S5_PAD, ts - 1)
        yc, yd, hssd, cbuf, sbuf = _ssd(padded(pad_t), state_ssd[l].reshape(bs, GROUP_W, SSD_STATE),
                                        state_ssd_conv[l], state_sconv[l], *ssd_args,
                                        bs, pad_t, pad_t, ts)
        unpad = lambda a: a.reshape(bs, -1, GROUP_W)[:, :ts].reshape(ns_tok, GROUP_W)

        qs = projs[:, COL_Q * GROUP_W:(COL_Q + 1) * GROUP_W].reshape(bs, ts, HEADS, HEAD_DIM)
        ks = projs[:, COL_K * GROUP_W:(COL_K + 1) * GROUP_W].reshape(bs, ts, HEADS, HEAD_DIM)
        vs = projs[:, COL_V * GROUP_W:(COL_V + 1) * GROUP_W].reshape(bs, ts, HEADS, HEAD_DIM)
        lfs = logfs[:, :HEADS].reshape(bs, ts, HEADS)
        qh = jnp.pad(jnp.swapaxes(qs * (FOX_SCALE * LOG2E), 1, 2), ((0, 0), (0, 0), (0, 8 - ts), (0, 0)))
        qbd = jnp.stack([jnp.pad(qh[:, h], ((0, 0), (0, 0), (h * HEAD_DIM, (HEADS - 1 - h) * HEAD_DIM)))
                         for h in range(HEADS)], axis=1).reshape(bs, QROWS, GROUP_W)
        as_page = lambda a: jnp.pad(jnp.transpose(a, (0, 2, 3, 1)).reshape(bs, GROUP_W, ts),
                                    ((0, 0), (0, 0), (0, PAGE - ts)))
        lnh = jnp.broadcast_to(jnp.swapaxes(lfs, 1, 2)[:, :, None, :], (bs, HEADS, 8, ts))
        lnh = jnp.pad(lnh.reshape(bs, QROWS, ts), ((0, 0), (0, 0), (0, 128 - ts)))
        rt = _past_bias(pt_flat, pool_t, l, bs, n_pages)
        oh = _fox_sample(pt_flat, ckt, cvt, qbd, as_page(ks), as_page(vs), rt, lnh, l, bs, n_pages, 16, ts)
        oh = oh.reshape(bs, HEADS, 8, HEADS, HEAD_DIM)[:, :, :ts]
        oh = jnp.stack([oh[:, h, :, h] for h in range(HEADS)], axis=2)
        yb = _rms_rows(oh.reshape(ns_tok, GROUP_W), fg)
        res = _out_ffn(hs, unpad(ya), yb, unpad(yc), unpad(yd), *ffn_args, ns_tok, final)
        hs = res[0]
        if final:
            y_sample = res[1]
        outs_s.append((
            ks, vs, lfs,
            jnp.stack([st5[:, 0, :S5_CH].reshape(bs, S5_G, S5_P),
                       st5[:, 0, S5_CH:].reshape(bs, S5_G, S5_P)], axis=-1),
            hssd.reshape(bs, HEADS, HEAD_DIM, SSD_STATE), cbuf, sbuf))

    stack = lambda outs: tuple(jnp.stack(col, axis=0) for col in zip(*outs))
    return ((y_prompt.reshape(bp, tp, D_MODEL), y_sample.reshape(bs, ts, D_MODEL))
            + stack(outs_p) + stack(outs_s))
```

```python
import functools
import math

import jax
import jax.numpy as jnp
from jax import lax
from jax.experimental import pallas as pl
from jax.experimental.pallas import tpu as pltpu

F32 = jnp.float32
BF16 = jnp.bfloat16

D_MODEL = 1024
DEPTH = 4
PAGE = 128
HEADS = 4
HEAD_DIM = 64
GROUP_W = 256
S5_G, S5_H, S5_P = 16, 16, 64
S5_CH = S5_G * S5_P
SSD_STATE = 128
SSD_CONV = 4
SC_CONV = 3
FFN = 4096
EPS = 1e-5
FOX_SCALE = HEAD_DIM ** -0.5
NEG = -1e30

COL_U, COL_Q, COL_K, COL_V, COL_Z, COL_XS, COL_B, COL_C, COL_SB, COL_SC, COL_SH = range(11)
MISC_OFF = 11 * GROUP_W
PROJ_W = MISC_OFF + 128
MISC_BLK = MISC_OFF // 128
DT_LANE = 4

VMEM_LIMIT = 56 * 1024 * 1024


def _cparams(sem):
    return pltpu.CompilerParams(dimension_semantics=sem, vmem_limit_bytes=VMEM_LIMIT)


def _const_spec(shape):
    nd = len(shape)
    return pl.BlockSpec(shape, lambda *_: (0,) * nd, pipeline_mode=pl.Buffered(1))


def _split3(x):
    hi = x.astype(BF16)
    r1 = x - hi.astype(F32)
    mid = r1.astype(BF16)
    lo = (r1 - mid.astype(F32)).astype(BF16)
    return hi, mid, lo


def _dot(a, b):
    return jnp.dot(a, b, preferred_element_type=F32)


def _dot_nt(a, b):
    return lax.dot_general(a, b, (((1,), (1,)), ((), ())), preferred_element_type=F32)


def _dot3_l(m_bf16, x):
    hi, mid, lo = _split3(x)
    return _dot(m_bf16, hi) + _dot(m_bf16, mid) + _dot(m_bf16, lo)


def _dot3_r(x, m_bf16):
    hi, mid, lo = _split3(x)
    return _dot(hi, m_bf16) + _dot(mid, m_bf16) + _dot(lo, m_bf16)


def _iota_mask(shape, fn):
    r = lax.broadcasted_iota(jnp.int32, shape, 0)
    c = lax.broadcasted_iota(jnp.int32, shape, 1)
    return fn(r, c)


def _tri_bf16(shape, fn):
    return jnp.where(_iota_mask(shape, fn), 1.0, 0.0).astype(BF16)


def _rms(x, g):
    return x * lax.rsqrt(jnp.mean(x * x, axis=-1, keepdims=True) + EPS) * g


def _sigmoid(x):
    return 1.0 / (1.0 + jnp.exp(-x))


def _softplus(x):
    return jnp.maximum(x, 0.0) + jnp.log1p(jnp.exp(-jnp.abs(x)))


def _lane_expand(cols, lane_head):
    out = jnp.where(lane_head == 2, cols[2], cols[3])
    out = jnp.where(lane_head == 1, cols[1], out)
    return jnp.where(lane_head == 0, cols[0], out)


def _in_proj_kernel(x_ref, g_ref, w_ref, o_ref):
    xn = _rms(x_ref[...], g_ref[...])
    o_ref[...] = _dot(xn.astype(BF16), w_ref[...])


def _in_proj(x, g, w, tm):
    n = x.shape[0]
    return pl.pallas_call(
        _in_proj_kernel,
        grid=(n // tm,),
        in_specs=[pl.BlockSpec((tm, D_MODEL), lambda i: (i, 0)),
                  _const_spec((1, D_MODEL)),
                  _const_spec((D_MODEL, PROJ_W))],
        out_specs=pl.BlockSpec((tm, PROJ_W), lambda i: (i, 0)),
        out_shape=jax.ShapeDtypeStruct((n, PROJ_W), F32),
        compiler_params=_cparams(("parallel",)),
        name="in_proj",
    )(x, g, w)


def _logf_cum_kernel(m_ref, b_ref, logf_ref, cum_ref, carry_ref, *, lc):
    @pl.when(pl.program_id(1) == 0)
    def _():
        carry_ref[...] = jnp.zeros_like(carry_ref)

    x = m_ref[...] + b_ref[...]
    logf = jnp.minimum(x, 0.0) - jnp.log1p(jnp.exp(-jnp.abs(x)))
    tri = _tri_bf16((lc, lc), lambda r, c: c <= r)
    cum = _dot3_l(tri, logf) + carry_ref[...]
    logf_ref[...] = logf
    cum_ref[...] = cum
    carry_ref[...] = cum[lc - 1:lc, :]


def _logf_cum(proj, bias, nseq, t, lc):
    nc = t // lc
    n = nseq * t
    return pl.pallas_call(
        functools.partial(_logf_cum_kernel, lc=lc),
        grid=(nseq, nc),
        in_specs=[pl.BlockSpec((lc, 128), lambda b, c: (b * nc + c, MISC_BLK)),
                  _const_spec((1, 128))],
        out_specs=[pl.BlockSpec((lc, 128), lambda b, c: (b * nc + c, 0)),
                   pl.BlockSpec((lc, 128), lambda b, c: (b * nc + c, 0))],
        out_shape=[jax.ShapeDtypeStruct((n, 128), F32), jax.ShapeDtypeStruct((n, 128), F32)],
        scratch_shapes=[pltpu.VMEM((1, 128), F32)],
        compiler_params=_cparams(("parallel", "arbitrary")),
        name="logf_cum",
    )(proj, bias)


AUG_W = HEADS * 128
LOG2E = math.log2(math.e)


def _bias_lanes(c, first):
    lane = lax.broadcasted_iota(jnp.int32, (1, AUG_W), 1)
    blk = lane // 128
    off = lane % 128 - HEAD_DIM
    hi, mid, lo = [t.astype(F32) for t in _split3(c)]
    out = jnp.where((off >= 3 - first) & (off < 6 - first), 1.0, 0.0)
    for j, t in enumerate((hi, mid, lo)):
        cols = [t[:, h:h + 1] for h in range(HEADS)]
        tl = jnp.where(blk == 2, cols[2], cols[3])
        tl = jnp.where(blk == 1, cols[1], tl)
        tl = jnp.where(blk == 0, cols[0], tl)
        out = jnp.where(off == first + j, tl, out)
    return out


def _fox_prep_kernel(m_ref, q_ref, k_ref, v_ref, b_ref, logf_ref, qa_ref, ka_ref, kt_ref, vt_ref,
                     vtb_ref, carry_ref, *, lc):
    @pl.when(pl.program_id(1) == 0)
    def _():
        carry_ref[...] = jnp.zeros_like(carry_ref)

    x = m_ref[...] + b_ref[...]
    logf = jnp.minimum(x, 0.0) - jnp.log1p(jnp.exp(-jnp.abs(x)))
    tri = _tri_bf16((lc, lc), lambda r, c: c <= r)
    cum = _dot3_l(tri, logf) + carry_ref[...]
    logf_ref[...] = logf
    carry_ref[...] = cum[lc - 1:lc, :]

    place = _tri_bf16((GROUP_W, AUG_W),
                      lambda r, c: (c // 128 == r // HEAD_DIM) & (c % 128 == r % HEAD_DIM))
    c2 = cum * LOG2E
    q = (q_ref[...] * (FOX_SCALE * LOG2E)).astype(BF16)
    qa_ref[...] = (_dot(q, place) + _bias_lanes(c2, 0)).astype(BF16)
    k = k_ref[...]
    ka_ref[...] = (_dot(k.astype(BF16), place) + _bias_lanes(-c2, 3)).astype(BF16)
    kt_ref[...] = jnp.transpose(k)
    vt = jnp.transpose(v_ref[...])
    vt_ref[...] = vt
    vtb_ref[...] = vt.astype(BF16)


def _fox_prep(proj, bias, nseq, t, lc):
    nc = t // lc
    n = nseq * t
    rows = lambda w, cb: pl.BlockSpec((lc, w), lambda b, c: (b * nc + c, cb))
    chan = pl.BlockSpec((None, GROUP_W, lc), lambda b, c: (b, 0, c))
    return pl.pallas_call(
        functools.partial(_fox_prep_kernel, lc=lc),
        grid=(nseq, nc),
        in_specs=[rows(128, MISC_BLK), rows(GROUP_W, COL_Q), rows(GROUP_W, COL_K), rows(GROUP_W, COL_V),
                  _const_spec((1, 128))],
        out_specs=[rows(128, 0), rows(AUG_W, 0), rows(AUG_W, 0), chan, chan, chan],
        out_shape=[jax.ShapeDtypeStruct((n, 128), F32), jax.ShapeDtypeStruct((n, AUG_W), BF16),
                   jax.ShapeDtypeStruct((n, AUG_W), BF16),
                   jax.ShapeDtypeStruct((nseq, GROUP_W, t), F32),
                   jax.ShapeDtypeStruct((nseq, GROUP_W, t), F32),
                   jax.ShapeDtypeStruct((nseq, GROUP_W, t), BF16)],
        scratch_shapes=[pltpu.VMEM((1, 128), F32)],
        compiler_params=_cparams(("parallel", "arbitrary")),
        name="fox_prep",
    )(proj, proj, proj, proj, bias)


def _s5_kernel(u_ref, h0_ref, bri_ref, ak_ref, ct_ref, d_ref, wg_ref, g_ref,
               y_ref, st_ref, bu_ref, xc_ref, *, tc, state_row):
    c = pl.program_id(1)

    @pl.when(c == 0)
    def _():
        h0 = h0_ref[...]
        xc_ref[0] = jnp.broadcast_to(h0[:, :S5_CH], (8, S5_CH))
        xc_ref[1] = jnp.broadcast_to(h0[:, S5_CH:], (8, S5_CH))

    u = u_ref[...]
    bu_ref[...] = _dot(u.astype(BF16), bri_ref[...])

    def tile(i, carry):
        xr_c, xi_c = carry
        r0 = pl.multiple_of(i * 8, 8)
        dr = bu_ref[pl.ds(r0, 8), 0:S5_CH]
        di = bu_ref[pl.ds(r0, 8), S5_CH:2 * S5_CH]
        for j, k in enumerate((1, 2, 4)):
            ar = ak_ref[2 + 2 * j]
            ai = ak_ref[3 + 2 * j]
            sr = pltpu.roll(dr, k, axis=0)
            si = pltpu.roll(di, k, axis=0)
            dr, di = dr + ar * sr - ai * si, di + ar * si + ai * sr
        pr = ak_ref[0]
        pi = ak_ref[1]
        xr = dr + pr * xr_c - pi * xi_c
        xi = di + pr * xi_c + pi * xr_c
        bu_ref[pl.ds(r0, 8), 0:S5_CH] = xr
        bu_ref[pl.ds(r0, 8), S5_CH:2 * S5_CH] = xi
        return (jnp.broadcast_to(xr[7:8, :], (8, S5_CH)),
                jnp.broadcast_to(xi[7:8, :], (8, S5_CH)))

    xr_c, xi_c = lax.fori_loop(0, tc // 8, tile, (xc_ref[0], xc_ref[1]))
    xc_ref[0] = xr_c
    xc_ref[1] = xi_c

    @pl.when(c == pl.num_programs(1) - 1)
    def _():
        st_ref[...] = bu_ref[state_row:state_row + 1, :]

    y = _dot(bu_ref[...].astype(BF16), ct_ref[...]) + d_ref[...] * u
    y = 0.5 * y * (1.0 + jnp.tanh(math.sqrt(2.0 / math.pi) * (y + 0.044715 * (y * y * y))))
    y = y * _sigmoid(_dot(y.astype(BF16), wg_ref[...]))
    y_ref[...] = _rms(y, g_ref[...])


def _s5(proj, h0, bri, ak, ct, d, wg, g, nseq, t, tc, state_row):
    nc = t // tc
    n = nseq * t
    return pl.pallas_call(
        functools.partial(_s5_kernel, tc=tc, state_row=state_row),
        grid=(nseq, nc),
        in_specs=[pl.BlockSpec((tc, GROUP_W), lambda b, c: (b * nc + c, COL_U)),
                  pl.BlockSpec((None, 1, 2 * S5_CH), lambda b, c: (b, 0, 0)),
                  _const_spec((GROUP_W, 2 * S5_CH)),
                  _const_spec((8, 8, S5_CH)),
                  _const_spec((2 * S5_CH, GROUP_W)),
                  _const_spec((1, GROUP_W)),
                  _const_spec((GROUP_W, GROUP_W)),
                  _const_spec((1, GROUP_W))],
        out_specs=[pl.BlockSpec((tc, GROUP_W), lambda b, c: (b * nc + c, 0)),
                   pl.BlockSpec((None, 1, 2 * S5_CH), lambda b, c: (b, 0, 0))],
        out_shape=[jax.ShapeDtypeStruct((n, GROUP_W), F32),
                   jax.ShapeDtypeStruct((nseq, 1, 2 * S5_CH), F32)],
        scratch_shapes=[pltpu.VMEM((tc, 2 * S5_CH), F32), pltpu.VMEM((2, 8, S5_CH), F32)],
        compiler_params=_cparams(("parallel", "arbitrary")),
        name="s5",
    )(proj, h0, bri, ak, ct, d, wg, g)


def _fox_prompt_kernel(qi_ref, ki_ref, q_ref, k_ref, vt_ref, g_ref, o_ref,
                       m_ref, l_ref, acc_ref, s_ref, *, tq, tk):
    step = pl.program_id(1)
    qi = qi_ref[step]
    ki = ki_ref[step]

    @pl.when(ki == 0)
    def _():
        m_ref[...] = jnp.full_like(m_ref, NEG)
        l_ref[...] = jnp.zeros_like(l_ref)
        acc_ref[...] = jnp.zeros_like(acc_ref)

    ratio = tq // tk
    diag = ki - qi * ratio

    def update(masked):
        if masked:
            causal = _iota_mask((tk, tq), lambda r, c: r + diag * tk <= c)
        for h in range(HEADS):
            s_ref[h] = _dot_nt(k_ref[:, h * 128:(h + 1) * 128], q_ref[:, h * 128:(h + 1) * 128])
        for h in range(HEADS):
            s = s_ref[h]
            if masked:
                s = jnp.where(causal, s, NEG)
            m_old = m_ref[h]
            m_new = jnp.maximum(m_old, jnp.max(s, axis=0, keepdims=True))
            alpha = jnp.exp2(m_old - m_new)
            p = jnp.exp2(s - m_new)
            l_ref[h] = alpha * l_ref[h] + jnp.sum(p, axis=0, keepdims=True)
            vth = vt_ref[h * HEAD_DIM:(h + 1) * HEAD_DIM, :]
            acc_ref[h] = alpha * acc_ref[h] + _dot(vth, p.astype(BF16))
            m_ref[h] = m_new

    @pl.when(diag < 0)
    def _():
        update(False)

    @pl.when(diag >= 0)
    def _():
        update(True)

    @pl.when(diag == ratio - 1)
    def _():
        parts = [acc_ref[h] / l_ref[h] for h in range(HEADS)]
        ot = jnp.concatenate(parts, axis=0)
        ot = ot * lax.rsqrt(jnp.mean(ot * ot, axis=0, keepdims=True) + EPS) * g_ref[...]
        o_ref[...] = jnp.transpose(ot)


def _fox_prompt(qa, ka, vtb, g, nseq, t, tq, tk):
    nq, nk = t // tq, t // tk
    n = nseq * t
    pairs = [(i, j) for i in range(nq) for j in range((i + 1) * (tq // tk))]
    qi_tab = jnp.asarray([p[0] for p in pairs], jnp.int32)
    ki_tab = jnp.asarray([p[1] for p in pairs], jnp.int32)
    grid_spec = pltpu.PrefetchScalarGridSpec(
        num_scalar_prefetch=2,
        grid=(nseq, len(pairs)),
        in_specs=[pl.BlockSpec((tq, AUG_W), lambda b, s, qt, kt: (b * nq + qt[s], 0)),
                  pl.BlockSpec((tk, AUG_W), lambda b, s, qt, kt: (b * nk + kt[s], 0)),
                  pl.BlockSpec((None, GROUP_W, tk), lambda b, s, qt, kt: (b, 0, kt[s])),
                  pl.BlockSpec((GROUP_W, 1), lambda b, s, qt, kt: (0, 0))],
        out_specs=pl.BlockSpec((tq, GROUP_W), lambda b, s, qt, kt: (b * nq + qt[s], 0)),
        scratch_shapes=[pltpu.VMEM((HEADS, 1, tq), F32),
                        pltpu.VMEM((HEADS, 1, tq), F32),
                        pltpu.VMEM((HEADS, HEAD_DIM, tq), F32),
                        pltpu.VMEM((HEADS, tk, tq), F32)],
    )
    return pl.pallas_call(
        functools.partial(_fox_prompt_kernel, tq=tq, tk=tk),
        grid_spec=grid_spec,
        out_shape=jax.ShapeDtypeStruct((n, GROUP_W), F32),
        compiler_params=_cparams(("parallel", "arbitrary")),
        name="fox_prompt",
    )(qi_tab, ki_tab, qa, ka, vtb, g)


PAGE_ROWS = PAGE * HEADS


def _past_bias_kernel(pt_ref, pool_ref, o_ref, lbuf_ref, sem, *, layer, n_pages):
    b = pl.program_id(0)
    slot = b % 2

    def page_copy(seq, sl, p):
        return pltpu.make_async_copy(pool_ref.at[layer, pt_ref[seq * n_pages + p]],
                                     lbuf_ref.at[sl, pl.ds(p * 8, 8)], sem.at[sl])

    def start_all(seq, sl):
        def body(p, _):
            page_copy(seq, sl, p).start()
            return 0
        lax.fori_loop(0, n_pages, body, 0)

    @pl.when(b == 0)
    def _():
        start_all(0, 0)

    @pl.when(b + 1 < pl.num_programs(0))
    def _():
        start_all(b + 1, 1 - slot)

    def wait(p, _):
        page_copy(b, slot, p).wait()
        return 0

    lax.fori_loop(0, n_pages, wait, 0)

    x = lbuf_ref[slot]
    rows = n_pages * 8
    after = _tri_bf16((PAGE, PAGE), lambda r, c: r > c)
    local = _dot3_r(x, after)
    ones = jnp.ones((PAGE, 128), BF16)
    tot = _dot3_r(x, ones)
    later = _tri_bf16((rows, rows), lambda r, c: (c % 8 == r % 8) & (c // 8 > r // 8))
    res = (local + _dot3_l(later, tot)) * LOG2E
    for p in range(n_pages):
        o_ref[:, p * PAGE:(p + 1) * PAGE] = res[p * 8:(p + 1) * 8, :]


def _past_bias(page_table_flat, pool_t, layer, nseq, n_pages):
    rows = n_pages * 8
    grid_spec = pltpu.PrefetchScalarGridSpec(
        num_scalar_prefetch=1,
        grid=(nseq,),
        in_specs=[pl.BlockSpec(memory_space=pl.ANY)],
        out_specs=pl.BlockSpec((None, 8, n_pages * PAGE), lambda b, pt: (b, 0, 0)),
        scratch_shapes=[pltpu.VMEM((2, rows, 128), F32), pltpu.SemaphoreType.DMA((2,))],
    )
    return pl.pallas_call(
        functools.partial(_past_bias_kernel, layer=layer, n_pages=n_pages),
        grid_spec=grid_spec,
        out_shape=jax.ShapeDtypeStruct((nseq, 8, n_pages * PAGE), F32),
        compiler_params=_cparams(("arbitrary",)),
        name="past_bias",
    )(page_table_flat, pool_t)


def _fox_sample_kernel(pt_ref, ck_ref, cv_ref, q_ref, kn_ref, vn_ref, rt_ref, ln_ref,
                       o_ref, kbuf, vbuf, sem, m_ref, l_ref, acc_ref,
                       *, layer, n_pages, pg, dec_t):
    g = pl.program_id(0)
    ng = pl.num_programs(0)
    nc = n_pages // pg
    c = g % nc
    slot = g % 2

    def copies(step, sl):
        out = []
        for j in range(pg):
            pid = pt_ref[step * pg + j]
            out.append(pltpu.make_async_copy(ck_ref.at[layer, pid], kbuf.at[sl, j], sem.at[0, sl]))
            out.append(pltpu.make_async_copy(cv_ref.at[layer, pid], vbuf.at[sl, j], sem.at[1, sl]))
        return out

    @pl.when(g == 0)
    def _():
        for cp in copies(0, 0):
            cp.start()

    @pl.when(g + 1 < ng)
    def _():
        for cp in copies(g + 1, 1 - slot):
            cp.start()

    for cp in copies(g, slot):
        cp.wait()

    row_tok = lax.broadcasted_iota(jnp.int32, (QROWS, 1), 0) % 8

    @pl.when(c == 0)
    def _():
        m_ref[...] = jnp.full_like(m_ref, NEG)
        l_ref[...] = jnp.zeros_like(l_ref)
        acc_ref[...] = jnp.zeros_like(acc_ref)

    ln = ln_ref[...] * LOG2E
    cums = [ln[:, 0:1]]
    for t in range(1, dec_t):
        cums.append(cums[-1] + ln[:, t:t + 1])
    pcol = cums[dec_t - 1]
    for t in range(dec_t - 2, -1, -1):
        pcol = jnp.where(row_tok == t, cums[t], pcol)

    qbd = q_ref[...].astype(BF16)

    def attend(s, vts):
        m_old = m_ref[...]
        m_new = jnp.maximum(m_old, jnp.max(s, axis=1, keepdims=True))
        alpha = jnp.exp2(m_old - m_new)
        p = jnp.exp2(s - m_new)
        l_ref[...] = alpha * l_ref[...] + jnp.sum(p, axis=1, keepdims=True)
        pb = p.astype(BF16)
        pv = [_dot_nt(pb[:, j * PAGE:(j + 1) * PAGE], vt) for j, vt in enumerate(vts)]
        while len(pv) > 1:
            pv = [a + b for a, b in zip(pv[0::2], pv[1::2])]
        acc_ref[...] = alpha * acc_ref[...] + pv[0]
        m_ref[...] = m_new

    rt = rt_ref[...]
    bias = jnp.concatenate([jnp.broadcast_to(rt[h:h + 1, :], (8, pg * PAGE)) for h in range(HEADS)], axis=0)
    s = jnp.concatenate([_dot(qbd, kbuf[slot, j].astype(BF16)) for j in range(pg)], axis=1)
    attend(s + bias + pcol, [vbuf[slot, j].astype(BF16) for j in range(pg)])

    @pl.when(c == nc - 1)
    def _():
        lane = lax.broadcasted_iota(jnp.int32, (1, PAGE), 1)
        pt_l = jnp.broadcast_to(cums[dec_t - 1], (QROWS, PAGE))
        for t in range(dec_t - 2, -1, -1):
            pt_l = jnp.where(lane == t, cums[t], pt_l)
        s_new = _dot(qbd, kn_ref[...].astype(BF16)) + pcol - pt_l
        s_new = jnp.where((lane < dec_t) & (lane <= row_tok), s_new, NEG)
        attend(s_new, [vn_ref[...].astype(BF16)])
        o_ref[...] = acc_ref[...] / l_ref[...]


QROWS = HEADS * 8


def _fox_sample(pt_flat, ckt, cvt, qbd, knt, vnt, rt, ln, layer, nseq, n_pages, pg, dec_t):
    nc = n_pages // pg
    hd = HEADS * HEAD_DIM
    per_seq = lambda shape: pl.BlockSpec((None,) + shape, lambda g, pt: (g // nc,) + (0,) * len(shape))
    grid_spec = pltpu.PrefetchScalarGridSpec(
        num_scalar_prefetch=1,
        grid=(nseq * nc,),
        in_specs=[pl.BlockSpec(memory_space=pl.ANY),
                  pl.BlockSpec(memory_space=pl.ANY),
                  per_seq((QROWS, hd)),
                  per_seq((hd, PAGE)),
                  per_seq((hd, PAGE)),
                  pl.BlockSpec((None, 8, pg * PAGE), lambda g, pt: (g // nc, 0, g % nc)),
                  per_seq((QROWS, 128))],
        out_specs=per_seq((QROWS, hd)),
        scratch_shapes=[pltpu.VMEM((2, pg, hd, PAGE), F32),
                        pltpu.VMEM((2, pg, hd, PAGE), F32),
                        pltpu.SemaphoreType.DMA((2, 2)),
                        pltpu.VMEM((QROWS, 1), F32),
                        pltpu.VMEM((QROWS, 1), F32),
                        pltpu.VMEM((QROWS, hd), F32)],
    )
    return pl.pallas_call(
        functools.partial(_fox_sample_kernel, layer=layer, n_pages=n_pages, pg=pg, dec_t=dec_t),
        grid_spec=grid_spec,
        out_shape=jax.ShapeDtypeStruct((nseq, QROWS, hd), F32),
        compiler_params=_cparams(("arbitrary",)),
        name="fox_sample",
    )(pt_flat, ckt, cvt, qbd, knt, vnt, rt, ln)


def _rms_rows_kernel(x_ref, g_ref, o_ref):
    o_ref[...] = _rms(x_ref[...], g_ref[...])


def _rms_rows(x, g):
    return pl.pallas_call(
        _rms_rows_kernel,
        out_shape=jax.ShapeDtypeStruct(x.shape, F32),
        name="rms_rows",
    )(x, g)


def _ssd_kernel(z_ref, xs_ref, b_ref, c_ref, sb_ref, sc_ref, sh_ref, misc_ref,
                h0_ref, cb0_ref, sb0_ref,
                cw_ref, cbias_ref, dtb_ref, alog_ref, dskip_ref, ng_ref, scw_ref, scg_ref,
                yc_ref, yd_ref, hout_ref, cbout_ref, sbout_ref,
                ext_ref, exts_ref, h_ref, *, L, valid):
    c = pl.program_id(1)
    last = pl.num_programs(1) - 1
    XW = 3 * GROUP_W

    @pl.when(c == 0)
    def _():
        h_ref[...] = h0_ref[...]
        ext_ref[8 - (SSD_CONV - 1):8, :] = cb0_ref[...]
        exts_ref[8 - (SC_CONV - 1):8, :] = sb0_ref[...]

    ext_ref[8:8 + L, 0:GROUP_W] = xs_ref[...]
    ext_ref[8:8 + L, GROUP_W:2 * GROUP_W] = b_ref[...]
    ext_ref[8:8 + L, 2 * GROUP_W:XW] = c_ref[...]
    exts_ref[8:8 + L, :] = sc_ref[...] * sh_ref[...]

    cw = cw_ref[...]
    acc = cbias_ref[...] + cw[0:1, :] * ext_ref[5:5 + L, :]
    for j in range(1, SSD_CONV):
        acc = acc + cw[j:j + 1, :] * ext_ref[5 + j:5 + j + L, :]
    xbc = acc * _sigmoid(acc)
    xs = xbc[:, 0:GROUP_W]
    bm = xbc[:, GROUP_W:2 * GROUP_W].astype(BF16)
    cm = xbc[:, 2 * GROUP_W:XW].astype(BF16)

    scw = scw_ref[...]
    conv = scw[0:1, :] * exts_ref[6:6 + L, :]
    for j in range(1, SC_CONV):
        conv = conv + scw[j:j + 1, :] * exts_ref[6 + j:6 + j + L, :]
    yd_ref[...] = _rms(sb_ref[...] * conv, scg_ref[...])

    @pl.when(c == last)
    def _():
        cbout_ref[...] = ext_ref[8 + valid - (SSD_CONV - 1):8 + valid, :]
        sbout_ref[...] = exts_ref[8 + valid - (SC_CONV - 1):8 + valid, :]

    ext_ref[8 - (SSD_CONV - 1):8, :] = ext_ref[8 + L - (SSD_CONV - 1):8 + L, :]
    exts_ref[8 - (SC_CONV - 1):8, :] = exts_ref[8 + L - (SC_CONV - 1):8 + L, :]

    lane128 = lax.broadcasted_iota(jnp.int32, (1, 128), 1)
    rowi = lax.broadcasted_iota(jnp.int32, (L, 1), 0)
    live = (lane128 >= DT_LANE) & (lane128 < DT_LANE + HEADS) & (rowi < valid)
    dt = jnp.where(live, _softplus(misc_ref[...] + dtb_ref[...]), 0.0)
    a = dt * (-jnp.exp(alog_ref[...]))
    incl = _tri_bf16((L, L), lambda r, cc: cc <= r)
    after = _tri_bf16((L, L), lambda r, cc: cc > r)
    acum = _dot3_l(incl, a)
    rev = _dot3_l(after, a)
    lower = _iota_mask((L, L), lambda r, cc: cc <= r)
    upto = _tri_bf16((L, L), lambda r, cc: r <= cc)
    acum_t = _dot3_r(jnp.transpose(a)[0:8, :], upto)

    lane_head = lax.broadcasted_iota(jnp.int32, (1, GROUP_W), 1) // HEAD_DIM
    col = lambda arr, h: arr[:, DT_LANE + h:DT_LANE + h + 1]
    dt_l = _lane_expand([col(dt, h) for h in range(HEADS)], lane_head)
    eac_l = _lane_expand([jnp.exp(col(acum, h)) for h in range(HEADS)], lane_head)
    erev_l = _lane_expand([jnp.exp(col(rev, h)) for h in range(HEADS)], lane_head)
    xdt = xs * dt_l
    xdt_b = xdt.astype(BF16)

    gmat = [_dot_nt(cm[:, g * SSD_STATE:(g + 1) * SSD_STATE], bm[:, g * SSD_STATE:(g + 1) * SSD_STATE])
            for g in range(2)]
    y = jnp.zeros((L, GROUP_W), F32)
    for h in range(HEADS):
        delta = col(acum, h) - acum_t[DT_LANE + h:DT_LANE + h + 1, :]
        lmat = jnp.exp(jnp.where(lower, delta, NEG))
        scores = (gmat[h // 2] * lmat).astype(BF16)
        y = jnp.where(lane_head == h, _dot(scores, xdt_b), y)

    h_all = h_ref[...]
    hb = h_all.astype(BF16)
    y0 = _dot_nt(cm[:, 0:SSD_STATE], hb)
    y1 = _dot_nt(cm[:, SSD_STATE:2 * SSD_STATE], hb)
    y = y + jnp.concatenate([y0[:, 0:128], y1[:, 128:256]], axis=1) * eac_l

    wt = jnp.transpose(xdt * erev_l).astype(BF16)
    s0 = _dot(wt, bm[:, 0:SSD_STATE])
    s1 = _dot(wt, bm[:, SSD_STATE:2 * SSD_STATE])
    upd = jnp.concatenate([s0[0:128, :], s1[128:256, :]], axis=0)
    row_head = lax.broadcasted_iota(jnp.int32, (GROUP_W, 1), 0) // HEAD_DIM
    dec = [jnp.exp(acum[L - 1:L, DT_LANE + h:DT_LANE + h + 1]) for h in range(HEADS)]
    scale = jnp.where(row_head == 2, dec[2], dec[3])
    scale = jnp.where(row_head == 1, dec[1], scale)
    scale = jnp.where(row_head == 0, dec[0], scale)
    h_new = scale * h_all + upd
    h_ref[...] = h_new

    @pl.when(c == last)
    def _():
        hout_ref[...] = h_new

    y = y + dskip_ref[...] * xs
    zz = z_ref[...]
    y = y * (zz * _sigmoid(zz))
    yc_ref[...] = _rms(y, ng_ref[...])


def _ssd(proj, h0, cb0, sb0, cw, cbias, dtb, alog, dskip, ng, scw, scg, nseq, t, L, valid):
    nc = t // L
    n = nseq * t
    XW = 3 * GROUP_W

    def col_spec(cb):
        return pl.BlockSpec((L, GROUP_W), lambda b, c: (b * nc + c, cb))

    state_specs = [pl.BlockSpec((None, GROUP_W, SSD_STATE), lambda b, c: (b, 0, 0)),
                   pl.BlockSpec((None, SSD_CONV - 1, XW), lambda b, c: (b, 0, 0)),
                   pl.BlockSpec((None, SC_CONV - 1, GROUP_W), lambda b, c: (b, 0, 0))]
    return pl.pallas_call(
        functools.partial(_ssd_kernel, L=L, valid=valid),
        grid=(nseq, nc),
        in_specs=[col_spec(COL_Z), col_spec(COL_XS), col_spec(COL_B), col_spec(COL_C),
                  col_spec(COL_SB), col_spec(COL_SC), col_spec(COL_SH),
                  pl.BlockSpec((L, 128), lambda b, c: (b * nc + c, MISC_BLK))]
                 + state_specs
                 + [_const_spec((SSD_CONV, XW)), _const_spec((1, XW)), _const_spec((1, 128)),
                    _const_spec((1, 128)), _const_spec((1, GROUP_W)), _const_spec((1, GROUP_W)),
                    _const_spec((SC_CONV, GROUP_W)), _const_spec((1, GROUP_W))],
        out_specs=[pl.BlockSpec((L, GROUP_W), lambda b, c: (b * nc + c, 0)),
                   pl.BlockSpec((L, GROUP_W), lambda b, c: (b * nc + c, 0))] + state_specs,
        out_shape=[jax.ShapeDtypeStruct((n, GROUP_W), F32),
                   jax.ShapeDtypeStruct((n, GROUP_W), F32),
                   jax.ShapeDtypeStruct((nseq, GROUP_W, SSD_STATE), F32),
                   jax.ShapeDtypeStruct((nseq, SSD_CONV - 1, XW), F32),
                   jax.ShapeDtypeStruct((nseq, SC_CONV - 1, GROUP_W), F32)],
        scratch_shapes=[pltpu.VMEM((8 + L, XW), F32), pltpu.VMEM((8 + L, GROUP_W), F32),
                        pltpu.VMEM((GROUP_W, SSD_STATE), F32)],
        compiler_params=_cparams(("parallel", "arbitrary")),
        name="ssd",
    )(proj, proj, proj, proj, proj, proj, proj, proj, h0, cb0, sb0,
      cw, cbias, dtb, alog, dskip, ng, scw, scg)


def _out_ffn_kernel(x_ref, ya_ref, yb_ref, yc_ref, yd_ref, wo_ref, g_ref, wu_ref, wd_ref, gf_ref,
                    *out_refs, final):
    x1 = x_ref[...]
    for i, y_ref in enumerate((ya_ref, yb_ref, yc_ref, yd_ref)):
        x1 = x1 + _dot(y_ref[...].astype(BF16), wo_ref[i * GROUP_W:(i + 1) * GROUP_W, :])
    hn = _rms(x1, g_ref[...]).astype(BF16)
    fc = 1024
    parts = []
    for j in range(FFN // fc):
        hj = jnp.maximum(_dot(hn, wu_ref[:, j * fc:(j + 1) * fc]), 0.0)
        parts.append(_dot((hj * hj).astype(BF16), wd_ref[j * fc:(j + 1) * fc, :]))
    acc = x1 + ((parts[0] + parts[1]) + (parts[2] + parts[3]))
    out_refs[0][...] = acc
    if final:
        out_refs[1][...] = _rms(acc, gf_ref[...])


def _out_ffn(x, ya, yb, yc, yd, wo, g, wu, wd, gf, tm, final):
    n = x.shape[0]
    row = lambda w: pl.BlockSpec((tm, w), lambda i: (i, 0))
    n_out = 2 if final else 1
    return pl.pallas_call(
        functools.partial(_out_ffn_kernel, final=final),
        grid=(n // tm,),
        in_specs=[row(D_MODEL), row(GROUP_W), row(GROUP_W), row(GROUP_W), row(GROUP_W),
                  _const_spec((D_MODEL, D_MODEL)), _const_spec((1, D_MODEL)),
                  _const_spec((D_MODEL, FFN)), _const_spec((FFN, D_MODEL)),
                  _const_spec((1, D_MODEL))],
        out_specs=[row(D_MODEL)] * n_out,
        out_shape=[jax.ShapeDtypeStruct((n, D_MODEL), F32)] * n_out,
        compiler_params=_cparams(("parallel",)),
        name="out_ffn",
    )(x, ya, yb, yc, yd, wo, g, wu, wd, gf)


def _s5_params(lam_re, lam_im, log_dt, b_re, b_im, c_re, c_im):
    dt = jnp.exp(log_dt)[:, None]
    mag = jnp.exp(lam_re * dt)
    abr, abi = mag * jnp.cos(lam_im * dt), mag * jnp.sin(lam_im * dt)
    den = lam_re * lam_re + lam_im * lam_im
    qr = ((abr - 1.0) * lam_re + abi * lam_im) / den
    qi = (abi * lam_re - (abr - 1.0) * lam_im) / den
    bbr = qr[..., None] * b_re - qi[..., None] * b_im
    bbi = qr[..., None] * b_im + qi[..., None] * b_re
    eye = jnp.eye(S5_G, dtype=F32)
    bd = lambda m: jnp.einsum('gph,gk->ghkp', m, eye).reshape(S5_G * S5_H, S5_CH)
    bri = jnp.concatenate([bd(bbr), bd(bbi)], axis=1).astype(BF16)
    cd = lambda m: jnp.einsum('ghp,gk->gpkh', m, eye).reshape(S5_CH, S5_G * S5_H)
    ct = jnp.concatenate([cd(c_re), -cd(c_im)], axis=0).astype(BF16)
    ar, ai = abr.reshape(1, S5_CH), abi.reshape(1, S5_CH)
    pows = [(ar, ai)]
    for _ in range(7):
        pr, pi = pows[-1]
        pows.append((pr * ar - pi * ai, pr * ai + pi * ar))
    rows = jnp.arange(8)[:, None]
    tabs = [jnp.concatenate([p[0] for p in pows], axis=0), jnp.concatenate([p[1] for p in pows], axis=0)]
    for k in (1, 2, 4):
        for part in pows[k - 1]:
            tabs.append(jnp.where(rows >= k, part, 0.0))
    return bri, jnp.stack(tabs, axis=0), ct


def _pad_lanes(v, off):
    return jnp.zeros((1, 128), F32).at[0, off:off + v.shape[0]].set(v)


def _permute_w_in(w):
    return jnp.concatenate(
        [w[:, 0:1024], w[:, 1028:2052], w[:, 2056:2824], w[:, 1024:1028], w[:, 2052:2056],
         jnp.zeros((w.shape[0], 120), w.dtype)], axis=1).astype(BF16)


def kernel(x_prompt, x_sample, cache_k, cache_v, cache_logf, state_s5, state_ssd, state_ssd_conv, state_sconv, page_table, norm_mix_g, w_in, s5_lam_re, s5_lam_im, s5_log_dt, s5_b_re, s5_b_im, s5_c_re, s5_c_im, s5_d, s5_w_glu, s5_norm_g, fox_b_f, fox_norm_g, ssd_conv_w, ssd_conv_b, ssd_dt_bias, ssd_a_log, ssd_d, ssd_norm_g, sc_conv_w, sc_norm_g, w_out, norm_ffn_g, w_up, w_down, norm_final_g):
    bp, tp, _ = x_prompt.shape
    bs, ts, _ = x_sample.shape
    n_pool = cache_k.shape[1]
    n_pages = page_table.shape[1]
    np_tok, ns_tok = bp * tp, bs * ts
    pad_t = 32
    S5_PAD = 8

    pt_flat = page_table.reshape(-1)
    ckt = jnp.transpose(cache_k, (0, 1, 3, 4, 2)).reshape(DEPTH, n_pool, GROUP_W, PAGE)
    cvt = jnp.transpose(cache_v, (0, 1, 3, 4, 2)).reshape(DEPTH, n_pool, GROUP_W, PAGE)
    pool_t = jnp.pad(jnp.swapaxes(cache_logf, 2, 3), ((0, 0), (0, 0), (0, 8 - HEADS), (0, 0)))
    gfin = norm_final_g.reshape(1, D_MODEL)

    hp = x_prompt.reshape(np_tok, D_MODEL)
    hs = x_sample.reshape(ns_tok, D_MODEL)
    zeros_p = dict(
        s5=jnp.zeros((bp, 1, 2 * S5_CH), F32), ssd=jnp.zeros((bp, GROUP_W, SSD_STATE), F32),
        cb=jnp.zeros((bp, SSD_CONV - 1, 3 * GROUP_W), F32), sb=jnp.zeros((bp, SC_CONV - 1, GROUP_W), F32))
    outs_p, outs_s = [], []
    y_prompt = y_sample = None

    for l in range(DEPTH):
        final = l == DEPTH - 1
        w_in_l = _permute_w_in(w_in[l])
        g_mix = norm_mix_g[l].reshape(1, D_MODEL)
        bri, ak, ct = _s5_params(s5_lam_re[l], s5_lam_im[l], s5_log_dt[l], s5_b_re[l], s5_b_im[l],
                                 s5_c_re[l], s5_c_im[l])
        s5_args = (bri, ak, ct, s5_d[l].reshape(1, GROUP_W), s5_w_glu[l].astype(BF16),
                   s5_norm_g[l].reshape(1, GROUP_W))
        fb = _pad_lanes(fox_b_f[l], 0)
        fg = fox_norm_g[l].reshape(1, GROUP_W)
        ssd_args = (ssd_conv_w[l], ssd_conv_b[l].reshape(1, -1), _pad_lanes(ssd_dt_bias[l], DT_LANE),
                    _pad_lanes(ssd_a_log[l], DT_LANE), jnp.repeat(ssd_d[l], HEAD_DIM).reshape(1, GROUP_W),
                    ssd_norm_g[l].reshape(1, GROUP_W), sc_conv_w[l], sc_norm_g[l].reshape(1, GROUP_W))
        ffn_args = (w_out[l].astype(BF16), norm_ffn_g[l].reshape(1, D_MODEL), w_up[l].astype(BF16),
                    w_down[l].astype(BF16), gfin)

        proj = _in_proj(hp, g_mix, w_in_l, 512)
        logf, qa, ka, kt, vt, vtb = _fox_prep(proj, fb, bp, tp, 512)
        ya, st5 = _s5(proj, zeros_p['s5'], *s5_args, bp, tp, 512, 511)
        yb = _fox_prompt(qa, ka, vtb, fox_norm_g[l].reshape(GROUP_W, 1), bp, tp, 512, 512)
        yc, yd, hssd, cbuf, sbuf = _ssd(proj, zeros_p['ssd'], zeros_p['cb'], zeros_p['sb'], *ssd_args,
                                        bp, tp, 256, 256)
        res = _out_ffn(hp, ya, yb, yc, yd, *ffn_args, 512, final)
        hp = res[0]
        if final:
            y_prompt = res[1]
        outs_p.append((
            jnp.transpose(kt.reshape(bp, HEADS, HEAD_DIM, tp), (0, 3, 1, 2)),
            jnp.transpose(vt.reshape(bp, HEADS, HEAD_DIM, tp), (0, 3, 1, 2)),
            logf[:, :HEADS].reshape(bp, tp, HEADS),
            jnp.stack([st5[:, 0, :S5_CH].reshape(bp, S5_G, S5_P),
                       st5[:, 0, S5_CH:].reshape(bp, S5_G, S5_P)], axis=-1),
            hssd.reshape(bp, HEADS, HEAD_DIM, SSD_STATE), cbuf, sbuf))

        projs = _in_proj(hs, g_mix, w_in_l, ns_tok)
        logfs, _ = _logf_cum(projs, fb, 1, ns_tok, ns_tok)
        padded = lambda r: jnp.pad(projs.reshape(bs, ts, PROJ_W),
                                   ((0, 0), (0, r - ts), (0, 0))).reshape(bs * r, PROJ_W)
        h0s5 = jnp.concatenate([state_s5[l][..., 0].reshape(bs, 1, S5_CH),
                                state_s5[l][..., 1].reshape(bs, 1, S5_CH)], axis=-1)
        ya, st5 = _s5(padded(S5_PAD), h0s5, *s5_args, bs, S5_PAD, S5_PAD, ts - 1)
        yc, yd, hssd, cbuf, sbuf = _ssd(padded(pad_t), state_ssd[l].reshape(bs, GROUP_W, SSD_STATE),
                                        state_ssd_conv[l], state_sconv[l], *ssd_args,
                                        bs, pad_t, pad_t, ts)
        unpad = lambda a: a.reshape(bs, -1, GROUP_W)[:, :ts].reshape(ns_tok, GROUP_W)

        qs = projs[:, COL_Q * GROUP_W:(COL_Q + 1) * GROUP_W].reshape(bs, ts, HEADS, HEAD_DIM)
        ks = projs[:, COL_K * GROUP_W:(COL_K + 1) * GROUP_W].reshape(bs, ts, HEADS, HEAD_DIM)
        vs = projs[:, COL_V * GROUP_W:(COL_V + 1) * GROUP_W].reshape(bs, ts, HEADS, HEAD_DIM)
        lfs = logfs[:, :HEADS].reshape(bs, ts, HEADS)
        qh = jnp.pad(jnp.swapaxes(qs * (FOX_SCALE * LOG2E), 1, 2), ((0, 0), (0, 0), (0, 8 - ts), (0, 0)))
        qbd = jnp.stack([jnp.pad(qh[:, h], ((0, 0), (0, 0), (h * HEAD_DIM, (HEADS - 1 - h) * HEAD_DIM)))
                         for h in range(HEADS)], axis=1).reshape(bs, QROWS, GROUP_W)
        as_page = lambda a: jnp.pad(jnp.transpose(a, (0, 2, 3, 1)).reshape(bs, GROUP_W, ts),
                                    ((0, 0), (0, 0), (0, PAGE - ts)))
        lnh = jnp.broadcast_to(jnp.swapaxes(lfs, 1, 2)[:, :, None, :], (bs, HEADS, 8, ts))
        lnh = jnp.pad(lnh.reshape(bs, QROWS, ts), ((0, 0), (0, 0), (0, 128 - ts)))
        rt = _past_bias(pt_flat, pool_t, l, bs, n_pages)
        oh = _fox_sample(pt_flat, ckt, cvt, qbd, as_page(ks), as_page(vs), rt, lnh, l, bs, n_pages, 32, ts)
        oh = oh.reshape(bs, HEADS, 8, HEADS, HEAD_DIM)[:, :, :ts]
        oh = jnp.stack([oh[:, h, :, h] for h in range(HEADS)], axis=2)
        yb = _rms_rows(oh.reshape(ns_tok, GROUP_W), fg)
        res = _out_ffn(hs, unpad(ya), yb, unpad(yc), unpad(yd), *ffn_args, ns_tok, final)
        hs = res[0]
        if final:
            y_sample = res[1]
        outs_s.append((
            ks, vs, lfs,
            jnp.stack([st5[:, 0, :S5_CH].reshape(bs, S5_G, S5_P),
                       st5[:, 0, S5_CH:].reshape(bs, S5_G, S5_P)], axis=-1),
            hssd.reshape(bs, HEADS, HEAD_DIM, SSD_STATE), cbuf, sbuf))

    stack = lambda outs: tuple(jnp.stack(col, axis=0) for col in zip(*outs))
    return ((y_prompt.reshape(bp, tp, D_MODEL), y_sample.reshape(bs, ts, D_MODEL))
            + stack(outs_p) + stack(outs_s))
```

```python
import functools
import math

import jax
import jax.numpy as jnp
from jax import lax
from jax.experimental import pallas as pl
from jax.experimental.pallas import tpu as pltpu

F32 = jnp.float32
BF16 = jnp.bfloat16

D_MODEL = 1024
DEPTH = 4
PAGE = 128
HEADS = 4
HEAD_DIM = 64
GROUP_W = 256
S5_G, S5_H, S5_P = 16, 16, 64
S5_CH = S5_G * S5_P
SSD_STATE = 128
SSD_CONV = 4
SC_CONV = 3
FFN = 4096
EPS = 1e-5
FOX_SCALE = HEAD_DIM ** -0.5
NEG = -1e30

COL_U, COL_Q, COL_K, COL_V, COL_Z, COL_XS, COL_B, COL_C, COL_SB, COL_SC, COL_SH = range(11)
MISC_OFF = 11 * GROUP_W
PROJ_W = MISC_OFF + 128
MISC_BLK = MISC_OFF // 128
DT_LANE = 4

VMEM_LIMIT = 56 * 1024 * 1024


def _cparams(sem):
    return pltpu.CompilerParams(dimension_semantics=sem, vmem_limit_bytes=VMEM_LIMIT)


def _const_spec(shape):
    nd = len(shape)
    return pl.BlockSpec(shape, lambda *_: (0,) * nd, pipeline_mode=pl.Buffered(1))


def _split3(x):
    hi = x.astype(BF16)
    r1 = x - hi.astype(F32)
    mid = r1.astype(BF16)
    lo = (r1 - mid.astype(F32)).astype(BF16)
    return hi, mid, lo


def _dot(a, b):
    return jnp.dot(a, b, preferred_element_type=F32)


def _dot_nt(a, b):
    return lax.dot_general(a, b, (((1,), (1,)), ((), ())), preferred_element_type=F32)


def _dot3_l(m_bf16, x):
    hi, mid, lo = _split3(x)
    return _dot(m_bf16, hi) + _dot(m_bf16, mid) + _dot(m_bf16, lo)


def _dot3_r(x, m_bf16):
    hi, mid, lo = _split3(x)
    return _dot(hi, m_bf16) + _dot(mid, m_bf16) + _dot(lo, m_bf16)


def _iota_mask(shape, fn):
    r = lax.broadcasted_iota(jnp.int32, shape, 0)
    c = lax.broadcasted_iota(jnp.int32, shape, 1)
    return fn(r, c)


def _tri_bf16(shape, fn):
    return jnp.where(_iota_mask(shape, fn), 1.0, 0.0).astype(BF16)


def _rms(x, g):
    return x * lax.rsqrt(jnp.mean(x * x, axis=-1, keepdims=True) + EPS) * g


def _sigmoid(x):
    return 1.0 / (1.0 + jnp.exp(-x))


def _softplus(x):
    return jnp.maximum(x, 0.0) + jnp.log1p(jnp.exp(-jnp.abs(x)))


def _lane_expand(cols, lane_head):
    out = jnp.where(lane_head == 2, cols[2], cols[3])
    out = jnp.where(lane_head == 1, cols[1], out)
    return jnp.where(lane_head == 0, cols[0], out)


def _in_proj_kernel(x_ref, g_ref, w_ref, o_ref):
    xn = _rms(x_ref[...], g_ref[...])
    o_ref[...] = _dot(xn.astype(BF16), w_ref[...])


def _in_proj(x, g, w, tm):
    n = x.shape[0]
    return pl.pallas_call(
        _in_proj_kernel,
        grid=(n // tm,),
        in_specs=[pl.BlockSpec((tm, D_MODEL), lambda i: (i, 0)),
                  _const_spec((1, D_MODEL)),
                  _const_spec((D_MODEL, PROJ_W))],
        out_specs=pl.BlockSpec((tm, PROJ_W), lambda i: (i, 0)),
        out_shape=jax.ShapeDtypeStruct((n, PROJ_W), F32),
        compiler_params=_cparams(("parallel",)),
        name="in_proj",
    )(x, g, w)


def _logf_cum_kernel(m_ref, b_ref, logf_ref, cum_ref, carry_ref, *, lc):
    @pl.when(pl.program_id(1) == 0)
    def _():
        carry_ref[...] = jnp.zeros_like(carry_ref)

    x = m_ref[...] + b_ref[...]
    logf = jnp.minimum(x, 0.0) - jnp.log1p(jnp.exp(-jnp.abs(x)))
    tri = _tri_bf16((lc, lc), lambda r, c: c <= r)
    cum = _dot3_l(tri, logf) + carry_ref[...]
    logf_ref[...] = logf
    cum_ref[...] = cum
    carry_ref[...] = cum[lc - 1:lc, :]


def _logf_cum(proj, bias, nseq, t, lc):
    nc = t // lc
    n = nseq * t
    return pl.pallas_call(
        functools.partial(_logf_cum_kernel, lc=lc),
        grid=(nseq, nc),
        in_specs=[pl.BlockSpec((lc, 128), lambda b, c: (b * nc + c, MISC_BLK)),
                  _const_spec((1, 128))],
        out_specs=[pl.BlockSpec((lc, 128), lambda b, c: (b * nc + c, 0)),
                   pl.BlockSpec((lc, 128), lambda b, c: (b * nc + c, 0))],
        out_shape=[jax.ShapeDtypeStruct((n, 128), F32), jax.ShapeDtypeStruct((n, 128), F32)],
        scratch_shapes=[pltpu.VMEM((1, 128), F32)],
        compiler_params=_cparams(("parallel", "arbitrary")),
        name="logf_cum",
    )(proj, bias)


AUG_W = HEADS * 128
LOG2E = math.log2(math.e)


def _bias_lanes(c, first):
    lane = lax.broadcasted_iota(jnp.int32, (1, AUG_W), 1)
    blk = lane // 128
    off = lane % 128 - HEAD_DIM
    hi, mid, lo = [t.astype(F32) for t in _split3(c)]
    out = jnp.where((off >= 3 - first) & (off < 6 - first), 1.0, 0.0)
    for j, t in enumerate((hi, mid, lo)):
        cols = [t[:, h:h + 1] for h in range(HEADS)]
        tl = jnp.where(blk == 2, cols[2], cols[3])
        tl = jnp.where(blk == 1, cols[1], tl)
        tl = jnp.where(blk == 0, cols[0], tl)
        out = jnp.where(off == first + j, tl, out)
    return out


def _fox_prep_kernel(m_ref, q_ref, k_ref, v_ref, b_ref, logf_ref, qa_ref, ka_ref, kt_ref, vt_ref,
                     vtb_ref, carry_ref, *, lc):
    @pl.when(pl.program_id(1) == 0)
    def _():
        carry_ref[...] = jnp.zeros_like(carry_ref)

    x = m_ref[...] + b_ref[...]
    logf = jnp.minimum(x, 0.0) - jnp.log1p(jnp.exp(-jnp.abs(x)))
    tri = _tri_bf16((lc, lc), lambda r, c: c <= r)
    cum = _dot3_l(tri, logf) + carry_ref[...]
    logf_ref[...] = logf
    carry_ref[...] = cum[lc - 1:lc, :]

    place = _tri_bf16((GROUP_W, AUG_W),
                      lambda r, c: (c // 128 == r // HEAD_DIM) & (c % 128 == r % HEAD_DIM))
    c2 = cum * LOG2E
    q = (q_ref[...] * (FOX_SCALE * LOG2E)).astype(BF16)
    qa_ref[...] = (_dot(q, place) + _bias_lanes(c2, 0)).astype(BF16)
    k = k_ref[...]
    ka_ref[...] = (_dot(k.astype(BF16), place) + _bias_lanes(-c2, 3)).astype(BF16)
    kt_ref[...] = jnp.transpose(k)
    vt = jnp.transpose(v_ref[...])
    vt_ref[...] = vt
    vtb_ref[...] = vt.astype(BF16)


def _fox_prep(proj, bias, nseq, t, lc):
    nc = t // lc
    n = nseq * t
    rows = lambda w, cb: pl.BlockSpec((lc, w), lambda b, c: (b * nc + c, cb))
    chan = pl.BlockSpec((None, GROUP_W, lc), lambda b, c: (b, 0, c))
    return pl.pallas_call(
        functools.partial(_fox_prep_kernel, lc=lc),
        grid=(nseq, nc),
        in_specs=[rows(128, MISC_BLK), rows(GROUP_W, COL_Q), rows(GROUP_W, COL_K), rows(GROUP_W, COL_V),
                  _const_spec((1, 128))],
        out_specs=[rows(128, 0), rows(AUG_W, 0), rows(AUG_W, 0), chan, chan, chan],
        out_shape=[jax.ShapeDtypeStruct((n, 128), F32), jax.ShapeDtypeStruct((n, AUG_W), BF16),
                   jax.ShapeDtypeStruct((n, AUG_W), BF16),
                   jax.ShapeDtypeStruct((nseq, GROUP_W, t), F32),
                   jax.ShapeDtypeStruct((nseq, GROUP_W, t), F32),
                   jax.ShapeDtypeStruct((nseq, GROUP_W, t), BF16)],
        scratch_shapes=[pltpu.VMEM((1, 128), F32)],
        compiler_params=_cparams(("parallel", "arbitrary")),
        name="fox_prep",
    )(proj, proj, proj, proj, bias)


def _s5_kernel(u_ref, h0_ref, bri_ref, ak_ref, ct_ref, d_ref, wg_ref, g_ref,
               y_ref, st_ref, bu_ref, xc_ref, *, tc, state_row):
    c = pl.program_id(1)

    @pl.when(c == 0)
    def _():
        h0 = h0_ref[...]
        xc_ref[0] = jnp.broadcast_to(h0[:, :S5_CH], (8, S5_CH))
        xc_ref[1] = jnp.broadcast_to(h0[:, S5_CH:], (8, S5_CH))

    u = u_ref[...]
    bu_ref[...] = _dot(u.astype(BF16), bri_ref[...])

    def tile(i, carry):
        xr_c, xi_c = carry
        r0 = pl.multiple_of(i * 8, 8)
        dr = bu_ref[pl.ds(r0, 8), 0:S5_CH]
        di = bu_ref[pl.ds(r0, 8), S5_CH:2 * S5_CH]
        for j, k in enumerate((1, 2, 4)):
            ar = ak_ref[2 + 2 * j]
            ai = ak_ref[3 + 2 * j]
            sr = pltpu.roll(dr, k, axis=0)
            si = pltpu.roll(di, k, axis=0)
            dr, di = dr + ar * sr - ai * si, di + ar * si + ai * sr
        pr = ak_ref[0]
        pi = ak_ref[1]
        xr = dr + pr * xr_c - pi * xi_c
        xi = di + pr * xi_c + pi * xr_c
        bu_ref[pl.ds(r0, 8), 0:S5_CH] = xr
        bu_ref[pl.ds(r0, 8), S5_CH:2 * S5_CH] = xi
        return (jnp.broadcast_to(xr[7:8, :], (8, S5_CH)),
                jnp.broadcast_to(xi[7:8, :], (8, S5_CH)))

    xr_c, xi_c = lax.fori_loop(0, tc // 8, tile, (xc_ref[0], xc_ref[1]))
    xc_ref[0] = xr_c
    xc_ref[1] = xi_c

    @pl.when(c == pl.num_programs(1) - 1)
    def _():
        st_ref[...] = bu_ref[state_row:state_row + 1, :]

    y = _dot(bu_ref[...].astype(BF16), ct_ref[...]) + d_ref[...] * u
    y = 0.5 * y * (1.0 + jnp.tanh(math.sqrt(2.0 / math.pi) * (y + 0.044715 * (y * y * y))))
    y = y * _sigmoid(_dot(y.astype(BF16), wg_ref[...]))
    y_ref[...] = _rms(y, g_ref[...])


def _s5(proj, h0, bri, ak, ct, d, wg, g, nseq, t, tc, state_row):
    nc = t // tc
    n = nseq * t
    return pl.pallas_call(
        functools.partial(_s5_kernel, tc=tc, state_row=state_row),
        grid=(nseq, nc),
        in_specs=[pl.BlockSpec((tc, GROUP_W), lambda b, c: (b * nc + c, COL_U)),
                  pl.BlockSpec((None, 1, 2 * S5_CH), lambda b, c: (b, 0, 0)),
                  _const_spec((GROUP_W, 2 * S5_CH)),
                  _const_spec((8, 8, S5_CH)),
                  _const_spec((2 * S5_CH, GROUP_W)),
                  _const_spec((1, GROUP_W)),
                  _const_spec((GROUP_W, GROUP_W)),
                  _const_spec((1, GROUP_W))],
        out_specs=[pl.BlockSpec((tc, GROUP_W), lambda b, c: (b * nc + c, 0)),
                   pl.BlockSpec((None, 1, 2 * S5_CH), lambda b, c: (b, 0, 0))],
        out_shape=[jax.ShapeDtypeStruct((n, GROUP_W), F32),
                   jax.ShapeDtypeStruct((nseq, 1, 2 * S5_CH), F32)],
        scratch_shapes=[pltpu.VMEM((tc, 2 * S5_CH), F32), pltpu.VMEM((2, 8, S5_CH), F32)],
        compiler_params=_cparams(("parallel", "arbitrary")),
        name="s5",
    )(proj, h0, bri, ak, ct, d, wg, g)


def _fox_prompt_kernel(qi_ref, ki_ref, q_ref, k_ref, vt_ref, g_ref, o_ref,
                       m_ref, l_ref, acc_ref, s_ref, *, tq, tk):
    step = pl.program_id(1)
    qi = qi_ref[step]
    ki = ki_ref[step]

    @pl.when(ki == 0)
    def _():
        m_ref[...] = jnp.full_like(m_ref, NEG)
        l_ref[...] = jnp.zeros_like(l_ref)
        acc_ref[...] = jnp.zeros_like(acc_ref)

    ratio = tq // tk
    diag = ki - qi * ratio

    def update(masked):
        if masked:
            shift = 0 if ratio == 1 else diag * tk
            causal = _iota_mask((tk, tq), lambda r, c: r + shift <= c)
        for h in range(HEADS):
            s_ref[h] = _dot_nt(k_ref[:, h * 128:(h + 1) * 128], q_ref[:, h * 128:(h + 1) * 128])
        for h in range(HEADS):
            s = s_ref[h]
            if masked:
                s = jnp.where(causal, s, NEG)
            m_old = m_ref[h]
            m_new = jnp.maximum(m_old, jnp.max(s, axis=0, keepdims=True))
            alpha = jnp.exp2(m_old - m_new)
            p = jnp.exp2(s - m_new)
            l_ref[h] = alpha * l_ref[h] + jnp.sum(p, axis=0, keepdims=True)
            vth = vt_ref[h * HEAD_DIM:(h + 1) * HEAD_DIM, :]
            acc_ref[h] = alpha * acc_ref[h] + _dot(vth, p.astype(BF16))
            m_ref[h] = m_new

    @pl.when(diag < 0)
    def _():
        update(False)

    @pl.when(diag >= 0)
    def _():
        update(True)

    @pl.when(diag == ratio - 1)
    def _():
        parts = [acc_ref[h] / l_ref[h] for h in range(HEADS)]
        ot = jnp.concatenate(parts, axis=0)
        ot = ot * lax.rsqrt(jnp.mean(ot * ot, axis=0, keepdims=True) + EPS) * g_ref[...]
        o_ref[...] = jnp.transpose(ot)


def _fox_prompt(qa, ka, vtb, g, nseq, t, tq, tk):
    nq, nk = t // tq, t // tk
    n = nseq * t
    pairs = [(i, j) for i in range(nq) for j in range((i + 1) * (tq // tk))]
    qi_tab = jnp.asarray([p[0] for p in pairs], jnp.int32)
    ki_tab = jnp.asarray([p[1] for p in pairs], jnp.int32)
    grid_spec = pltpu.PrefetchScalarGridSpec(
        num_scalar_prefetch=2,
        grid=(nseq, len(pairs)),
        in_specs=[pl.BlockSpec((tq, AUG_W), lambda b, s, qt, kt: (b * nq + qt[s], 0)),
                  pl.BlockSpec((tk, AUG_W), lambda b, s, qt, kt: (b * nk + kt[s], 0)),
                  pl.BlockSpec((None, GROUP_W, tk), lambda b, s, qt, kt: (b, 0, kt[s])),
                  pl.BlockSpec((GROUP_W, 1), lambda b, s, qt, kt: (0, 0))],
        out_specs=pl.BlockSpec((tq, GROUP_W), lambda b, s, qt, kt: (b * nq + qt[s], 0)),
        scratch_shapes=[pltpu.VMEM((HEADS, 1, tq), F32),
                        pltpu.VMEM((HEADS, 1, tq), F32),
                        pltpu.VMEM((HEADS, HEAD_DIM, tq), F32),
                        pltpu.VMEM((HEADS, tk, tq), F32)],
    )
    return pl.pallas_call(
        functools.partial(_fox_prompt_kernel, tq=tq, tk=tk),
        grid_spec=grid_spec,
        out_shape=jax.ShapeDtypeStruct((n, GROUP_W), F32),
        compiler_params=_cparams(("parallel", "arbitrary")),
        name="fox_prompt",
    )(qi_tab, ki_tab, qa, ka, vtb, g)


PAGE_ROWS = PAGE * HEADS


def _past_bias_kernel(pt_ref, pool_ref, o_ref, lbuf_ref, sem, *, layer, n_pages):
    b = pl.program_id(0)
    slot = b % 2

    def page_copy(seq, sl, p):
        return pltpu.make_async_copy(pool_ref.at[layer, pt_ref[seq * n_pages + p]],
                                     lbuf_ref.at[sl, pl.ds(p * 8, 8)], sem.at[sl])

    def start_all(seq, sl):
        def body(p, _):
            page_copy(seq, sl, p).start()
            return 0
        lax.fori_loop(0, n_pages, body, 0)

    @pl.when(b == 0)
    def _():
        start_all(0, 0)

    @pl.when(b + 1 < pl.num_programs(0))
    def _():
        start_all(b + 1, 1 - slot)

    def wait(p, _):
        page_copy(b, slot, p).wait()
        return 0

    lax.fori_loop(0, n_pages, wait, 0)

    x = lbuf_ref[slot]
    rows = n_pages * 8
    after = _tri_bf16((PAGE, PAGE), lambda r, c: r > c)
    local = _dot3_r(x, after)
    ones = jnp.ones((PAGE, 128), BF16)
    tot = _dot3_r(x, ones)
    later = _tri_bf16((rows, rows), lambda r, c: (c % 8 == r % 8) & (c // 8 > r // 8))
    res = (local + _dot3_l(later, tot)) * LOG2E
    for p in range(n_pages):
        o_ref[:, p * PAGE:(p + 1) * PAGE] = res[p * 8:(p + 1) * 8, :]


def _past_bias(page_table_flat, pool_t, layer, nseq, n_pages):
    rows = n_pages * 8
    grid_spec = pltpu.PrefetchScalarGridSpec(
        num_scalar_prefetch=1,
        grid=(nseq,),
        in_specs=[pl.BlockSpec(memory_space=pl.ANY)],
        out_specs=pl.BlockSpec((None, 8, n_pages * PAGE), lambda b, pt: (b, 0, 0)),
        scratch_shapes=[pltpu.VMEM((2, rows, 128), F32), pltpu.SemaphoreType.DMA((2,))],
    )
    return pl.pallas_call(
        functools.partial(_past_bias_kernel, layer=layer, n_pages=n_pages),
        grid_spec=grid_spec,
        out_shape=jax.ShapeDtypeStruct((nseq, 8, n_pages * PAGE), F32),
        compiler_params=_cparams(("arbitrary",)),
        name="past_bias",
    )(page_table_flat, pool_t)


def _fox_sample_kernel(pt_ref, ck_ref, cv_ref, q_ref, kn_ref, vn_ref, rt_ref, ln_ref,
                       o_ref, kbuf, vbuf, sem, m_ref, l_ref, acc_ref,
                       *, layer, n_pages, pg, dec_t):
    g = pl.program_id(0)
    ng = pl.num_programs(0)
    nc = n_pages // pg
    c = g % nc
    slot = g % 2

    def copies(step, sl):
        out = []
        for j in range(pg):
            pid = pt_ref[step * pg + j]
            out.append(pltpu.make_async_copy(ck_ref.at[layer, pid], kbuf.at[sl, j], sem.at[0, sl]))
            out.append(pltpu.make_async_copy(cv_ref.at[layer, pid], vbuf.at[sl, j], sem.at[1, sl]))
        return out

    @pl.when(g == 0)
    def _():
        for cp in copies(0, 0):
            cp.start()

    @pl.when(g + 1 < ng)
    def _():
        for cp in copies(g + 1, 1 - slot):
            cp.start()

    for cp in copies(g, slot):
        cp.wait()

    row_tok = lax.broadcasted_iota(jnp.int32, (QROWS, 1), 0) % 8

    @pl.when(c == 0)
    def _():
        m_ref[...] = jnp.full_like(m_ref, NEG)
        l_ref[...] = jnp.zeros_like(l_ref)
        acc_ref[...] = jnp.zeros_like(acc_ref)

    ln = ln_ref[...] * LOG2E
    cums = [ln[:, 0:1]]
    for t in range(1, dec_t):
        cums.append(cums[-1] + ln[:, t:t + 1])
    pcol = cums[dec_t - 1]
    for t in range(dec_t - 2, -1, -1):
        pcol = jnp.where(row_tok == t, cums[t], pcol)

    qbd = q_ref[...].astype(BF16)

    def attend(s, vts):
        m_old = m_ref[...]
        m_new = jnp.maximum(m_old, jnp.max(s, axis=1, keepdims=True))
        alpha = jnp.exp2(m_old - m_new)
        p = jnp.exp2(s - m_new)
        l_ref[...] = alpha * l_ref[...] + jnp.sum(p, axis=1, keepdims=True)
        pb = p.astype(BF16)
        pv = [_dot_nt(pb[:, j * PAGE:(j + 1) * PAGE], vt) for j, vt in enumerate(vts)]
        while len(pv) > 1:
            pv = [a + b for a, b in zip(pv[0::2], pv[1::2])]
        acc_ref[...] = alpha * acc_ref[...] + pv[0]
        m_ref[...] = m_new

    rt = rt_ref[...]
    bias = jnp.concatenate([jnp.broadcast_to(rt[h:h + 1, :], (8, pg * PAGE)) for h in range(HEADS)], axis=0)
    s = jnp.concatenate([_dot(qbd, kbuf[slot, j].astype(BF16)) for j in range(pg)], axis=1)
    attend(s + bias + pcol, [vbuf[slot, j].astype(BF16) for j in range(pg)])

    @pl.when(c == nc - 1)
    def _():
        lane = lax.broadcasted_iota(jnp.int32, (1, PAGE), 1)
        pt_l = jnp.broadcast_to(cums[dec_t - 1], (QROWS, PAGE))
        for t in range(dec_t - 2, -1, -1):
            pt_l = jnp.where(lane == t, cums[t], pt_l)
        s_new = _dot(qbd, kn_ref[...].astype(BF16)) + pcol - pt_l
        s_new = jnp.where((lane < dec_t) & (lane <= row_tok), s_new, NEG)
        attend(s_new, [vn_ref[...].astype(BF16)])
        o_ref[...] = acc_ref[...] / l_ref[...]


QROWS = HEADS * 8


def _fox_sample(pt_flat, ckt, cvt, qbd, knt, vnt, rt, ln, layer, nseq, n_pages, pg, dec_t):
    nc = n_pages // pg
    hd = HEADS * HEAD_DIM
    per_seq = lambda shape: pl.BlockSpec((None,) + shape, lambda g, pt: (g // nc,) + (0,) * len(shape))
    grid_spec = pltpu.PrefetchScalarGridSpec(
        num_scalar_prefetch=1,
        grid=(nseq * nc,),
        in_specs=[pl.BlockSpec(memory_space=pl.ANY),
                  pl.BlockSpec(memory_space=pl.ANY),
                  per_seq((QROWS, hd)),
                  per_seq((hd, PAGE)),
                  per_seq((hd, PAGE)),
                  pl.BlockSpec((None, 8, pg * PAGE), lambda g, pt: (g // nc, 0, g % nc)),
                  per_seq((QROWS, 128))],
        out_specs=per_seq((QROWS, hd)),
        scratch_shapes=[pltpu.VMEM((2, pg, hd, PAGE), F32),
                        pltpu.VMEM((2, pg, hd, PAGE), F32),
                        pltpu.SemaphoreType.DMA((2, 2)),
                        pltpu.VMEM((QROWS, 1), F32),
                        pltpu.VMEM((QROWS, 1), F32),
                        pltpu.VMEM((QROWS, hd), F32)],
    )
    return pl.pallas_call(
        functools.partial(_fox_sample_kernel, layer=layer, n_pages=n_pages, pg=pg, dec_t=dec_t),
        grid_spec=grid_spec,
        out_shape=jax.ShapeDtypeStruct((nseq, QROWS, hd), F32),
        compiler_params=_cparams(("arbitrary",)),
        name="fox_sample",
    )(pt_flat, ckt, cvt, qbd, knt, vnt, rt, ln)


def _rms_rows_kernel(x_ref, g_ref, o_ref):
    o_ref[...] = _rms(x_ref[...], g_ref[...])


def _rms_rows(x, g):
    return pl.pallas_call(
        _rms_rows_kernel,
        out_shape=jax.ShapeDtypeStruct(x.shape, F32),
        name="rms_rows",
    )(x, g)


def _ssd_kernel(z_ref, xs_ref, b_ref, c_ref, sb_ref, sc_ref, sh_ref, misc_ref,
                h0_ref, cb0_ref, sb0_ref,
                cw_ref, cbias_ref, dtb_ref, alog_ref, dskip_ref, ng_ref, scw_ref, scg_ref,
                yc_ref, yd_ref, hout_ref, cbout_ref, sbout_ref,
                ext_ref, exts_ref, h_ref, *, L, valid):
    c = pl.program_id(1)
    last = pl.num_programs(1) - 1
    XW = 3 * GROUP_W

    @pl.when(c == 0)
    def _():
        h_ref[...] = h0_ref[...]
        ext_ref[8 - (SSD_CONV - 1):8, :] = cb0_ref[...]
        exts_ref[8 - (SC_CONV - 1):8, :] = sb0_ref[...]

    ext_ref[8:8 + L, 0:GROUP_W] = xs_ref[...]
    ext_ref[8:8 + L, GROUP_W:2 * GROUP_W] = b_ref[...]
    ext_ref[8:8 + L, 2 * GROUP_W:XW] = c_ref[...]
    exts_ref[8:8 + L, :] = sc_ref[...] * sh_ref[...]

    cw = cw_ref[...]
    acc = cbias_ref[...] + cw[0:1, :] * ext_ref[5:5 + L, :]
    for j in range(1, SSD_CONV):
        acc = acc + cw[j:j + 1, :] * ext_ref[5 + j:5 + j + L, :]
    xbc = acc * _sigmoid(acc)
    xs = xbc[:, 0:GROUP_W]
    bm = xbc[:, GROUP_W:2 * GROUP_W].astype(BF16)
    cm = xbc[:, 2 * GROUP_W:XW].astype(BF16)

    scw = scw_ref[...]
    conv = scw[0:1, :] * exts_ref[6:6 + L, :]
    for j in range(1, SC_CONV):
        conv = conv + scw[j:j + 1, :] * exts_ref[6 + j:6 + j + L, :]
    yd_ref[...] = _rms(sb_ref[...] * conv, scg_ref[...])

    @pl.when(c == last)
    def _():
        cbout_ref[...] = ext_ref[8 + valid - (SSD_CONV - 1):8 + valid, :]
        sbout_ref[...] = exts_ref[8 + valid - (SC_CONV - 1):8 + valid, :]

    ext_ref[8 - (SSD_CONV - 1):8, :] = ext_ref[8 + L - (SSD_CONV - 1):8 + L, :]
    exts_ref[8 - (SC_CONV - 1):8, :] = exts_ref[8 + L - (SC_CONV - 1):8 + L, :]

    lane128 = lax.broadcasted_iota(jnp.int32, (1, 128), 1)
    rowi = lax.broadcasted_iota(jnp.int32, (L, 1), 0)
    live = (lane128 >= DT_LANE) & (lane128 < DT_LANE + HEADS) & (rowi < valid)
    dt = jnp.where(live, _softplus(misc_ref[...] + dtb_ref[...]), 0.0)
    a = dt * (-jnp.exp(alog_ref[...]))
    incl = _tri_bf16((L, L), lambda r, cc: cc <= r)
    after = _tri_bf16((L, L), lambda r, cc: cc > r)
    acum = _dot3_l(incl, a)
    rev = _dot3_l(after, a)
    lower = _iota_mask((L, L), lambda r, cc: cc <= r)
    upto = _tri_bf16((L, L), lambda r, cc: r <= cc)
    acum_t = _dot3_r(jnp.transpose(a)[0:8, :], upto)

    lane_head = lax.broadcasted_iota(jnp.int32, (1, GROUP_W), 1) // HEAD_DIM
    col = lambda arr, h: arr[:, DT_LANE + h:DT_LANE + h + 1]
    dt_l = _lane_expand([col(dt, h) for h in range(HEADS)], lane_head)
    eac_l = _lane_expand([jnp.exp(col(acum, h)) for h in range(HEADS)], lane_head)
    erev_l = _lane_expand([jnp.exp(col(rev, h)) for h in range(HEADS)], lane_head)
    xdt = xs * dt_l
    xdt_b = xdt.astype(BF16)

    gmat = [_dot_nt(cm[:, g * SSD_STATE:(g + 1) * SSD_STATE], bm[:, g * SSD_STATE:(g + 1) * SSD_STATE])
            for g in range(2)]
    y = jnp.zeros((L, GROUP_W), F32)
    for h in range(HEADS):
        delta = col(acum, h) - acum_t[DT_LANE + h:DT_LANE + h + 1, :]
        lmat = jnp.exp(jnp.where(lower, delta, NEG))
        scores = (gmat[h // 2] * lmat).astype(BF16)
        y = jnp.where(lane_head == h, _dot(scores, xdt_b), y)

    h_all = h_ref[...]
    hb = h_all.astype(BF16)
    y0 = _dot_nt(cm[:, 0:SSD_STATE], hb)
    y1 = _dot_nt(cm[:, SSD_STATE:2 * SSD_STATE], hb)
    y = y + jnp.concatenate([y0[:, 0:128], y1[:, 128:256]], axis=1) * eac_l

    wt = jnp.transpose(xdt * erev_l).astype(BF16)
    s0 = _dot(wt, bm[:, 0:SSD_STATE])
    s1 = _dot(wt, bm[:, SSD_STATE:2 * SSD_STATE])
    upd = jnp.concatenate([s0[0:128, :], s1[128:256, :]], axis=0)
    row_head = lax.broadcasted_iota(jnp.int32, (GROUP_W, 1), 0) // HEAD_DIM
    dec = [jnp.exp(acum[L - 1:L, DT_LANE + h:DT_LANE + h + 1]) for h in range(HEADS)]
    scale = jnp.where(row_head == 2, dec[2], dec[3])
    scale = jnp.where(row_head == 1, dec[1], scale)
    scale = jnp.where(row_head == 0, dec[0], scale)
    h_new = scale * h_all + upd
    h_ref[...] = h_new

    @pl.when(c == last)
    def _():
        hout_ref[...] = h_new

    y = y + dskip_ref[...] * xs
    zz = z_ref[...]
    y = y * (zz * _sigmoid(zz))
    yc_ref[...] = _rms(y, ng_ref[...])


def _ssd(proj, h0, cb0, sb0, cw, cbias, dtb, alog, dskip, ng, scw, scg, nseq, t, L, valid):
    nc = t // L
    n = nseq * t
    XW = 3 * GROUP_W

    def col_spec(cb):
        return pl.BlockSpec((L, GROUP_W), lambda b, c: (b * nc + c, cb))

    state_specs = [pl.BlockSpec((None, GROUP_W, SSD_STATE), lambda b, c: (b, 0, 0)),
                   pl.BlockSpec((None, SSD_CONV - 1, XW), lambda b, c: (b, 0, 0)),
                   pl.BlockSpec((None, SC_CONV - 1, GROUP_W), lambda b, c: (b, 0, 0))]
    return pl.pallas_call(
        functools.partial(_ssd_kernel, L=L, valid=valid),
        grid=(nseq, nc),
        in_specs=[col_spec(COL_Z), col_spec(COL_XS), col_spec(COL_B), col_spec(COL_C),
                  col_spec(COL_SB), col_spec(COL_SC), col_spec(COL_SH),
                  pl.BlockSpec((L, 128), lambda b, c: (b * nc + c, MISC_BLK))]
                 + state_specs
                 + [_const_spec((SSD_CONV, XW)), _const_spec((1, XW)), _const_spec((1, 128)),
                    _const_spec((1, 128)), _const_spec((1, GROUP_W)), _const_spec((1, GROUP_W)),
                    _const_spec((SC_CONV, GROUP_W)), _const_spec((1, GROUP_W))],
        out_specs=[pl.BlockSpec((L, GROUP_W), lambda b, c: (b * nc + c, 0)),
                   pl.BlockSpec((L, GROUP_W), lambda b, c: (b * nc + c, 0))] + state_specs,
        out_shape=[jax.ShapeDtypeStruct((n, GROUP_W), F32),
                   jax.ShapeDtypeStruct((n, GROUP_W), F32),
                   jax.ShapeDtypeStruct((nseq, GROUP_W, SSD_STATE), F32),
                   jax.ShapeDtypeStruct((nseq, SSD_CONV - 1, XW), F32),
                   jax.ShapeDtypeStruct((nseq, SC_CONV - 1, GROUP_W), F32)],
        scratch_shapes=[pltpu.VMEM((8 + L, XW), F32), pltpu.VMEM((8 + L, GROUP_W), F32),
                        pltpu.VMEM((GROUP_W, SSD_STATE), F32)],
        compiler_params=_cparams(("parallel", "arbitrary")),
        name="ssd",
    )(proj, proj, proj, proj, proj, proj, proj, proj, h0, cb0, sb0,
      cw, cbias, dtb, alog, dskip, ng, scw, scg)


def _out_ffn_kernel(x_ref, ya_ref, yb_ref, yc_ref, yd_ref, wo_ref, g_ref, wu_ref, wd_ref, gf_ref,
                    *out_refs, final):
    x1 = x_ref[...]
    for i, y_ref in enumerate((ya_ref, yb_ref, yc_ref, yd_ref)):
        x1 = x1 + _dot(y_ref[...].astype(BF16), wo_ref[i * GROUP_W:(i + 1) * GROUP_W, :])
    hn = _rms(x1, g_ref[...]).astype(BF16)
    fc = 1024
    parts = []
    for j in range(FFN // fc):
        hj = jnp.maximum(_dot(hn, wu_ref[:, j * fc:(j + 1) * fc]), 0.0)
        parts.append(_dot((hj * hj).astype(BF16), wd_ref[j * fc:(j + 1) * fc, :]))
    acc = x1 + ((parts[0] + parts[1]) + (parts[2] + parts[3]))
    out_refs[0][...] = acc
    if final:
        out_refs[1][...] = _rms(acc, gf_ref[...])


def _out_ffn(x, ya, yb, yc, yd, wo, g, wu, wd, gf, tm, final):
    n = x.shape[0]
    row = lambda w: pl.BlockSpec((tm, w), lambda i: (i, 0))
    n_out = 2 if final else 1
    return pl.pallas_call(
        functools.partial(_out_ffn_kernel, final=final),
        grid=(n // tm,),
        in_specs=[row(D_MODEL), row(GROUP_W), row(GROUP_W), row(GROUP_W), row(GROUP_W),
                  _const_spec((D_MODEL, D_MODEL)), _const_spec((1, D_MODEL)),
                  _const_spec((D_MODEL, FFN)), _const_spec((FFN, D_MODEL)),
                  _const_spec((1, D_MODEL))],
        out_specs=[row(D_MODEL)] * n_out,
        out_shape=[jax.ShapeDtypeStruct((n, D_MODEL), F32)] * n_out,
        compiler_params=_cparams(("parallel",)),
        name="out_ffn",
    )(x, ya, yb, yc, yd, wo, g, wu, wd, gf)


def _s5_params(lam_re, lam_im, log_dt, b_re, b_im, c_re, c_im):
    dt = jnp.exp(log_dt)[:, None]
    mag = jnp.exp(lam_re * dt)
    abr, abi = mag * jnp.cos(lam_im * dt), mag * jnp.sin(lam_im * dt)
    den = lam_re * lam_re + lam_im * lam_im
    qr = ((abr - 1.0) * lam_re + abi * lam_im) / den
    qi = (abi * lam_re - (abr - 1.0) * lam_im) / den
    bbr = qr[..., None] * b_re - qi[..., None] * b_im
    bbi = qr[..., None] * b_im + qi[..., None] * b_re
    eye = jnp.eye(S5_G, dtype=F32)
    bd = lambda m: jnp.einsum('gph,gk->ghkp', m, eye).reshape(S5_G * S5_H, S5_CH)
    bri = jnp.concatenate([bd(bbr), bd(bbi)], axis=1).astype(BF16)
    cd = lambda m: jnp.einsum('ghp,gk->gpkh', m, eye).reshape(S5_CH, S5_G * S5_H)
    ct = jnp.concatenate([cd(c_re), -cd(c_im)], axis=0).astype(BF16)
    ar, ai = abr.reshape(1, S5_CH), abi.reshape(1, S5_CH)
    pows = [(ar, ai)]
    for _ in range(7):
        pr, pi = pows[-1]
        pows.append((pr * ar - pi * ai, pr * ai + pi * ar))
    rows = jnp.arange(8)[:, None]
    tabs = [jnp.concatenate([p[0] for p in pows], axis=0), jnp.concatenate([p[1] for p in pows], axis=0)]
    for k in (1, 2, 4):
        for part in pows[k - 1]:
            tabs.append(jnp.where(rows >= k, part, 0.0))
    return bri, jnp.stack(tabs, axis=0), ct


def _pad_lanes(v, off):
    return jnp.zeros((1, 128), F32).at[0, off:off + v.shape[0]].set(v)


def _permute_w_in(w):
    return jnp.concatenate(
        [w[:, 0:1024], w[:, 1028:2052], w[:, 2056:2824], w[:, 1024:1028], w[:, 2052:2056],
         jnp.zeros((w.shape[0], 120), w.dtype)], axis=1).astype(BF16)


def kernel(x_prompt, x_sample, cache_k, cache_v, cache_logf, state_s5, state_ssd, state_ssd_conv, state_sconv, page_table, norm_mix_g, w_in, s5_lam_re, s5_lam_im, s5_log_dt, s5_b_re, s5_b_im, s5_c_re, s5_c_im, s5_d, s5_w_glu, s5_norm_g, fox_b_f, fox_norm_g, ssd_conv_w, ssd_conv_b, ssd_dt_bias, ssd_a_log, ssd_d, ssd_norm_g, sc_conv_w, sc_norm_g, w_out, norm_ffn_g, w_up, w_down, norm_final_g):
    bp, tp, _ = x_prompt.shape
    bs, ts, _ = x_sample.shape
    n_pool = cache_k.shape[1]
    n_pages = page_table.shape[1]
    np_tok, ns_tok = bp * tp, bs * ts
    pad_t = 32
    S5_PAD = 8

    pt_flat = page_table.reshape(-1)
    ckt = jnp.transpose(cache_k, (0, 1, 3, 4, 2)).reshape(DEPTH, n_pool, GROUP_W, PAGE)
    cvt = jnp.transpose(cache_v, (0, 1, 3, 4, 2)).reshape(DEPTH, n_pool, GROUP_W, PAGE)
    pool_t = jnp.pad(jnp.swapaxes(cache_logf, 2, 3), ((0, 0), (0, 0), (0, 8 - HEADS), (0, 0)))
    gfin = norm_final_g.reshape(1, D_MODEL)

    hp = x_prompt.reshape(np_tok, D_MODEL)
    hs = x_sample.reshape(ns_tok, D_MODEL)
    zeros_p = dict(
        s5=jnp.zeros((bp, 1, 2 * S5_CH), F32), ssd=jnp.zeros((bp, GROUP_W, SSD_STATE), F32),
        cb=jnp.zeros((bp, SSD_CONV - 1, 3 * GROUP_W), F32), sb=jnp.zeros((bp, SC_CONV - 1, GROUP_W), F32))
    outs_p, outs_s = [], []
    y_prompt = y_sample = None

    for l in range(DEPTH):
        final = l == DEPTH - 1
        w_in_l = _permute_w_in(w_in[l])
        g_mix = norm_mix_g[l].reshape(1, D_MODEL)
        bri, ak, ct = _s5_params(s5_lam_re[l], s5_lam_im[l], s5_log_dt[l], s5_b_re[l], s5_b_im[l],
                                 s5_c_re[l], s5_c_im[l])
        s5_args = (bri, ak, ct, s5_d[l].reshape(1, GROUP_W), s5_w_glu[l].astype(BF16),
                   s5_norm_g[l].reshape(1, GROUP_W))
        fb = _pad_lanes(fox_b_f[l], 0)
        fg = fox_norm_g[l].reshape(1, GROUP_W)
        ssd_args = (ssd_conv_w[l], ssd_conv_b[l].reshape(1, -1), _pad_lanes(ssd_dt_bias[l], DT_LANE),
                    _pad_lanes(ssd_a_log[l], DT_LANE), jnp.repeat(ssd_d[l], HEAD_DIM).reshape(1, GROUP_W),
                    ssd_norm_g[l].reshape(1, GROUP_W), sc_conv_w[l], sc_norm_g[l].reshape(1, GROUP_W))
        ffn_args = (w_out[l].astype(BF16), norm_ffn_g[l].reshape(1, D_MODEL), w_up[l].astype(BF16),
                    w_down[l].astype(BF16), gfin)

        proj = _in_proj(hp, g_mix, w_in_l, 512)
        logf, qa, ka, kt, vt, vtb = _fox_prep(proj, fb, bp, tp, 512)
        ya, st5 = _s5(proj, zeros_p['s5'], *s5_args, bp, tp, 1024, 1023)
        yb = _fox_prompt(qa, ka, vtb, fox_norm_g[l].reshape(GROUP_W, 1), bp, tp, 512, 512)
        yc, yd, hssd, cbuf, sbuf = _ssd(proj, zeros_p['ssd'], zeros_p['cb'], zeros_p['sb'], *ssd_args,
                                        bp, tp, 256, 256)
        res = _out_ffn(hp, ya, yb, yc, yd, *ffn_args, 512, final)
        hp = res[0]
        if final:
            y_prompt = res[1]
        outs_p.append((
            jnp.transpose(kt.reshape(bp, HEADS, HEAD_DIM, tp), (0, 3, 1, 2)),
            jnp.transpose(vt.reshape(bp, HEADS, HEAD_DIM, tp), (0, 3, 1, 2)),
            logf[:, :HEADS].reshape(bp, tp, HEADS),
            jnp.stack([st5[:, 0, :S5_CH].reshape(bp, S5_G, S5_P),
                       st5[:, 0, S5_CH:].reshape(bp, S5_G, S5_P)], axis=-1),
            hssd.reshape(bp, HEADS, HEAD_DIM, SSD_STATE), cbuf, sbuf))

        projs = _in_proj(hs, g_mix, w_in_l, ns_tok)
        logfs, _ = _logf_cum(projs, fb, 1, ns_tok, ns_tok)
        padded = lambda r: jnp.pad(projs.reshape(bs, ts, PROJ_W),
                                   ((0, 0), (0, r - ts), (0, 0))).reshape(bs * r, PROJ_W)
        h0s5 = jnp.concatenate([state_s5[l][..., 0].reshape(bs, 1, S5_CH),
                                state_s5[l][..., 1].reshape(bs, 1, S5_CH)], axis=-1)
        ya, st5 = _s5(padded(S5_PAD), h0s5, *s5_args, bs, S5_PAD, S5_PAD, ts - 1)
        yc, yd, hssd, cbuf, sbuf = _ssd(padded(pad_t), state_ssd[l].reshape(bs, GROUP_W, SSD_STATE),
                                        state_ssd_conv[l], state_sconv[l], *ssd_args,
                                        bs, pad_t, pad_t, ts)
        unpad = lambda a: a.reshape(bs, -1, GROUP_W)[:, :ts].reshape(ns_tok, GROUP_W)

        qs = projs[:, COL_Q * GROUP_W:(COL_Q + 1) * GROUP_W].reshape(bs, ts, HEADS, HEAD_DIM)
        ks = projs[:, COL_K * GROUP_W:(COL_K + 1) * GROUP_W].reshape(bs, ts, HEADS, HEAD_DIM)
        vs = projs[:, COL_V * GROUP_W:(COL_V + 1) * GROUP_W].reshape(bs, ts, HEADS, HEAD_DIM)
        lfs = logfs[:, :HEADS].reshape(bs, ts, HEADS)
        qh = jnp.pad(jnp.swapaxes(qs * (FOX_SCALE * LOG2E), 1, 2), ((0, 0), (0, 0), (0, 8 - ts), (0, 0)))
        qbd = jnp.stack([jnp.pad(qh[:, h], ((0, 0), (0, 0), (h * HEAD_DIM, (HEADS - 1 - h) * HEAD_DIM)))
                         for h in range(HEADS)], axis=1).reshape(bs, QROWS, GROUP_W)
        as_page = lambda a: jnp.pad(jnp.transpose(a, (0, 2, 3, 1)).reshape(bs, GROUP_W, ts),
                                    ((0, 0), (0, 0), (0, PAGE - ts)))
        lnh = jnp.broadcast_to(jnp.swapaxes(lfs, 1, 2)[:, :, None, :], (bs, HEADS, 8, ts))
        lnh = jnp.pad(lnh.reshape(bs, QROWS, ts), ((0, 0), (0, 0), (0, 128 - ts)))
        rt = _past_bias(pt_flat, pool_t, l, bs, n_pages)
        oh = _fox_sample(pt_flat, ckt, cvt, qbd, as_page(ks), as_page(vs), rt, lnh, l, bs, n_pages, 32, ts)
        oh = oh.reshape(bs, HEADS, 8, HEADS, HEAD_DIM)[:, :, :ts]
        oh = jnp.stack([oh[:, h, :, h] for h in range(HEADS)], axis=2)
        yb = _rms_rows(oh.reshape(ns_tok, GROUP_W), fg)
        res = _out_ffn(hs, unpad(ya), yb, unpad(yc), unpad(yd), *ffn_args, ns_tok, final)
        hs = res[0]
        if final:
            y_sample = res[1]
        outs_s.append((
            ks, vs, lfs,
            jnp.stack([st5[:, 0, :S5_CH].reshape(bs, S5_G, S5_P),
                       st5[:, 0, S5_CH:].reshape(bs, S5_G, S5_P)], axis=-1),
            hssd.reshape(bs, HEADS, HEAD_DIM, SSD_STATE), cbuf, sbuf))

    stack = lambda outs: tuple(jnp.stack(col, axis=0) for col in zip(*outs))
    return ((y_prompt.reshape(bp, tp, D_MODEL), y_sample.reshape(bs, ts, D_MODEL))
            + stack(outs_p) + stack(outs_s))
```
